```python
import math
import jax, jax.numpy as jnp
from jax import lax
import numpy as np

D_MODEL = 1024
BATCH = 8
SEQ = 2048
DEPTH = 2
DEC_BATCH = 128
DEC_SEQ = 8
PAST_LEN = 16384
PAGE_SIZE = 128

HEAD_A = 64
C_A = D_MODEL // 2
H_A = C_A // HEAD_A
LORA_W = 64
LORA_A = 64
LORA_G = 128
C_B = D_MODEL // 2
CONV_W = 31
C_C = D_MODEL // 2
GROUP_C = 16
G_C = C_C // GROUP_C
P_C = 64
N_BRANCH = 3
D_FF = 2816
FFN_CONV_W = 3
RMS_EPS = 1e-6
LN_EPS = 1e-5
GN_EPS = 64e-5

N_RWKV_IN = 3 * C_A + LORA_W + LORA_A + LORA_G
N_IN = N_RWKV_IN + 2 * C_B + C_C + N_BRANCH * D_MODEL

kernel_name = "rwkv7_conformer_s5_gated_hybrid_step"


def rmsnorm(x, g):
    x32 = x.astype(jnp.float32)
    y = x32 * lax.rsqrt(jnp.mean(x32 * x32, axis=-1, keepdims=True) + RMS_EPS)
    return y.astype(x.dtype) * g


def causal_dwconv(u, buf, w, b):
    full = jnp.concatenate([buf.astype(u.dtype), u], axis=1)
    out = lax.conv_general_dilated(
        full, w.astype(u.dtype)[:, None, :], window_strides=(1,), padding="VALID",
        dimension_numbers=("NWC", "WIO", "NWC"), feature_group_count=u.shape[-1])
    new_buf = full[:, -(w.shape[0] - 1):].astype(buf.dtype)
    return out + b, new_buf


def wkv_scan(r, w, k, v, a, b, s0):
    def step(s, inp):
        r_t, w_t, k_t, v_t, a_t, b_t = inp
        sa = jnp.einsum("bhij,bhj->bhi", s, a_t)
        s = (s * w_t[:, :, None, :] + sa[..., None] * b_t[:, :, None, :]
             + v_t[..., None] * k_t[:, :, None, :])
        return s, jnp.einsum("bhij,bhj->bhi", s, r_t)
    xs = tuple(jnp.moveaxis(t, 1, 0) for t in (r, w, k, v, a, b))
    s_fin, ys = lax.scan(step, s0, xs)
    return jnp.moveaxis(ys, 0, 1), s_fin


def rwkv7_branch(p, shift_prev, wkv, mu, w0, w_up, a0, a_up, g_up, k_k, k_a, r_k, ln_g, ln_b, w_o):
    B, L, _ = p.shape
    f32 = jnp.float32
    prev = jnp.concatenate([shift_prev[:, None].astype(p.dtype), p[:, :-1]], axis=1)
    xm = p + (prev - p) * mu
    r, k, v, xw, xa, xg = jnp.split(
        xm, [C_A, 2 * C_A, 3 * C_A, 3 * C_A + LORA_W, 3 * C_A + LORA_W + LORA_A], axis=-1)
    wlog = -jax.nn.softplus(-(w0 + jnp.tanh(xw) @ w_up).astype(f32)) - 0.5
    decay = jnp.exp(-jnp.exp(wlog))
    a = jax.nn.sigmoid((a0 + xa @ a_up).astype(f32))
    g = jax.nn.sigmoid(xg) @ g_up
    heads = lambda t: t.reshape(B, L, H_A, HEAD_A)
    k32 = k.astype(f32)
    kk = heads(k32 * k_k)
    kk = kk * lax.rsqrt(jnp.maximum(jnp.sum(kk * kk, axis=-1, keepdims=True), 1e-24))
    k_h = heads(k32 * (1.0 + (a - 1.0) * k_a))
    a_h = heads(a)
    r_h = heads(r.astype(f32))
    v_h = heads(v.astype(f32))
    y, wkv_new = wkv_scan(r_h, heads(decay), k_h, v_h, -kk, kk * a_h, wkv.astype(f32))
    mean = jnp.mean(y, axis=-1, keepdims=True)
    var = jnp.mean(jnp.square(y - mean), axis=-1, keepdims=True)
    y = ((y - mean) * lax.rsqrt(var + GN_EPS)).reshape(B, L, C_A) * ln_g + ln_b
    bonus = (jnp.sum(r_h * k_h * r_k, axis=-1, keepdims=True) * v_h).reshape(B, L, C_A)
    y = (y + bonus).astype(p.dtype) * g
    return y @ w_o, p[:, -1].astype(shift_prev.dtype), wkv_new.astype(wkv.dtype)


def conformer_branch(p, buf, dw_w, dw_b, ln_g, ln_b, w_o):
    u = p[..., :C_B] * jax.nn.sigmoid(p[..., C_B:])
    c, new_buf = causal_dwconv(u, buf, dw_w, dw_b)
    c32 = c.astype(jnp.float32)
    mean = jnp.mean(c32, axis=-1, keepdims=True)
    var = jnp.mean(jnp.square(c32 - mean), axis=-1, keepdims=True)
    n = ((c32 - mean) * lax.rsqrt(var + LN_EPS)).astype(p.dtype) * ln_g + ln_b
    return jax.nn.silu(n) @ w_o, new_buf


def _complex_scan_op(e1, e2):
    a1r, a1i, b1r, b1i = e1
    a2r, a2i, b2r, b2i = e2
    return (a2r * a1r - a2i * a1i, a2r * a1i + a2i * a1r,
            a2r * b1r - a2i * b1i + b2r, a2r * b1i + a2i * b1r + b2i)


def s5_branch(u, h_re, h_im, lam_re, lam_im, log_dt, b_re, b_im, c_re, c_im, d, w_glu):
    B, L, _ = u.shape
    f32 = jnp.float32
    dt = jnp.exp(log_dt.astype(f32))[:, None]
    lr = lam_re.astype(f32)
    li = lam_im.astype(f32)
    mag = jnp.exp(lr * dt)
    ab_re = mag * jnp.cos(li * dt)
    ab_im = mag * jnp.sin(li * dt)
    den = lr * lr + li * li
    q_re = ((ab_re - 1.0) * lr + ab_im * li) / den
    q_im = (ab_im * lr - (ab_re - 1.0) * li) / den
    ug = u.astype(f32).reshape(B, L, G_C, GROUP_C)
    bu_re = jnp.einsum("blgc,gpc->blgp", ug, b_re.astype(f32))
    bu_im = jnp.einsum("blgc,gpc->blgp", ug, b_im.astype(f32))
    in_re = q_re * bu_re - q_im * bu_im
    in_im = q_re * bu_im + q_im * bu_re
    h0r = h_re.astype(f32)
    h0i = h_im.astype(f32)
    in_re = in_re.at[:, 0].add(ab_re * h0r - ab_im * h0i)
    in_im = in_im.at[:, 0].add(ab_re * h0i + ab_im * h0r)
    a_re = jnp.broadcast_to(ab_re, in_re.shape)
    a_im = jnp.broadcast_to(ab_im, in_im.shape)
    _, _, s_re, s_im = lax.associative_scan(_complex_scan_op, (a_re, a_im, in_re, in_im), axis=1)
    y = (jnp.einsum("blgp,gcp->blgc", s_re, c_re.astype(f32))
         - jnp.einsum("blgp,gcp->blgc", s_im, c_im.astype(f32)))
    y = (y.reshape(B, L, C_C) + d * u.astype(f32)).astype(u.dtype)
    z = y @ w_glu
    out = z[..., :D_MODEL] * jax.nn.sigmoid(z[..., D_MODEL:])
    return out, s_re[:, -1].astype(h_re.dtype), s_im[:, -1].astype(h_im.dtype)


def mixer_block(xn, shift, wkv, conv, ssm_re, ssm_im, prm):
    B, L, _ = xn.shape
    p = xn @ prm["w_in"]
    o1 = N_RWKV_IN
    o2 = o1 + 2 * C_B
    o3 = o2 + C_C
    y_a, shift_n, wkv_n = rwkv7_branch(
        p[..., :o1], shift, wkv, prm["rwkv_mu"], prm["rwkv_w0"], prm["rwkv_w_up"], prm["rwkv_a0"],
        prm["rwkv_a_up"], prm["rwkv_g_up"], prm["rwkv_k_k"], prm["rwkv_k_a"], prm["rwkv_r_k"],
        prm["rwkv_ln_g"], prm["rwkv_ln_b"], prm["rwkv_w_out"])
    y_b, conv_n = conformer_branch(
        p[..., o1:o2], conv, prm["conf_dw_w"], prm["conf_dw_b"], prm["conf_ln_g"], prm["conf_ln_b"],
        prm["conf_w_out"])
    y_c, re_n, im_n = s5_branch(
        p[..., o2:o3], ssm_re, ssm_im, prm["s5_lambda_re"], prm["s5_lambda_im"], prm["s5_log_dt"],
        prm["s5_b_re"], prm["s5_b_im"], prm["s5_c_re"], prm["s5_c_im"], prm["s5_d"], prm["s5_w_glu"])
    gates = jax.nn.sigmoid(p[..., o3:] + prm["b_gate"]).reshape(B, L, N_BRANCH, D_MODEL)
    h = gates[:, :, 0] * y_a + gates[:, :, 1] * y_b + gates[:, :, 2] * y_c
    return h @ prm["w_out"], shift_n, wkv_n, conv_n, re_n, im_n


def conv_ffn(xn, buf, w_up, dw_w, dw_b, w_down):
    up = xn @ w_up
    c, new_buf = causal_dwconv(up, buf, dw_w, dw_b)
    return (jax.nn.silu(c[..., :D_FF]) * c[..., D_FF:]) @ w_down, new_buf


def run_trunk(x, shift, wkv, conv, ssm_re, ssm_im, ffn_conv, layers, norm_final_g):
    outs = ([], [], [], [], [], [])
    for l in range(DEPTH):
        prm = layers[l]
        m, s1, s2, s3, s4, s5 = mixer_block(
            rmsnorm(x, prm["norm_mix_g"]), shift[l], wkv[l], conv[l], ssm_re[l], ssm_im[l], prm)
        x = x + m
        f, s6 = conv_ffn(rmsnorm(x, prm["norm_ffn_g"]), ffn_conv[l], prm["ffn_w_up"],
                         prm["ffn_dw_w"], prm["ffn_dw_b"], prm["ffn_w_down"])
        x = x + f
        for lst, s in zip(outs, (s1, s2, s3, s4, s5, s6)):
            lst.append(s)
    y = rmsnorm(x, norm_final_g)
    return y, [jnp.stack(lst) for lst in outs]


def setup_inputs(seed: int = 0) -> dict:
    key = jax.random.key(seed)
    ks = iter(jax.random.split(key, 64))

    def nrm(shape, scale):
        return scale * jax.random.normal(next(ks), shape, jnp.float32)

    def uni(shape, lo, hi):
        return jax.random.uniform(next(ks), shape, jnp.float32, lo, hi)

    L = DEPTH
    lam_im = jnp.pi * jnp.arange(P_C, dtype=jnp.float32)
    return {
        "x_prompt": nrm((BATCH, SEQ, D_MODEL), 1.0),
        "x_sample": nrm((DEC_BATCH, DEC_SEQ, D_MODEL), 1.0),
        "state_shift": nrm((L, DEC_BATCH, N_RWKV_IN), 1.0),
        "state_wkv": nrm((L, DEC_BATCH, H_A, HEAD_A, HEAD_A), 1.0),
        "state_conv": nrm((L, DEC_BATCH, CONV_W - 1, C_B), 0.5),
        "state_ssm_re": nrm((L, DEC_BATCH, G_C, P_C), 0.1),
        "state_ssm_im": nrm((L, DEC_BATCH, G_C, P_C), 0.1),
        "state_ffn_conv": nrm((L, DEC_BATCH, FFN_CONV_W - 1, 2 * D_FF), 1.0),
        "norm_mix_g": 1.0 + nrm((L, D_MODEL), 0.02),
        "w_in": nrm((L, D_MODEL, N_IN), D_MODEL ** -0.5),
        "b_gate": nrm((L, N_BRANCH * D_MODEL), 0.02),
        "rwkv_mu": uni((L, N_RWKV_IN), 0.0, 1.0),
        "rwkv_w0": uni((L, C_A), -6.0, 1.0),
        "rwkv_w_up": nrm((L, LORA_W, C_A), 0.1 * LORA_W ** -0.5),
        "rwkv_a0": nrm((L, C_A), 0.1),
        "rwkv_a_up": nrm((L, LORA_A, C_A), 0.5 * LORA_A ** -0.5),
        "rwkv_g_up": nrm((L, LORA_G, C_A), LORA_G ** -0.5),
        "rwkv_k_k": 0.85 + nrm((L, C_A), 0.02),
        "rwkv_k_a": 1.0 + nrm((L, C_A), 0.02),
        "rwkv_r_k": nrm((L, H_A, HEAD_A), 0.1),
        "rwkv_ln_g": 1.0 + nrm((L, C_A), 0.02),
        "rwkv_ln_b": nrm((L, C_A), 0.02),
        "rwkv_w_out": nrm((L, C_A, D_MODEL), C_A ** -0.5),
        "conf_dw_w": nrm((L, CONV_W, C_B), CONV_W ** -0.5),
        "conf_dw_b": nrm((L, C_B), 0.02),
        "conf_ln_g": 1.0 + nrm((L, C_B), 0.02),
        "conf_ln_b": nrm((L, C_B), 0.02),
        "conf_w_out": nrm((L, C_B, D_MODEL), C_B ** -0.5),
        "s5_lambda_re": -0.5 + nrm((L, G_C, P_C), 0.01),
        "s5_lambda_im": lam_im + nrm((L, G_C, P_C), 0.01),
        "s5_log_dt": uni((L, G_C), math.log(1e-3), math.log(1e-1)),
        "s5_b_re": nrm((L, G_C, P_C, GROUP_C), GROUP_C ** -0.5),
        "s5_b_im": nrm((L, G_C, P_C, GROUP_C), GROUP_C ** -0.5),
        "s5_c_re": nrm((L, G_C, GROUP_C, P_C), P_C ** -0.5),
        "s5_c_im": nrm((L, G_C, GROUP_C, P_C), P_C ** -0.5),
        "s5_d": nrm((L, C_C), 1.0),
        "s5_w_glu": nrm((L, C_C, 2 * D_MODEL), C_C ** -0.5),
        "w_out": nrm((L, D_MODEL, D_MODEL), D_MODEL ** -0.5),
        "norm_ffn_g": 1.0 + nrm((L, D_MODEL), 0.02),
        "ffn_w_up": nrm((L, D_MODEL, 2 * D_FF), D_MODEL ** -0.5),
        "ffn_dw_w": nrm((L, FFN_CONV_W, 2 * D_FF), FFN_CONV_W ** -0.5),
        "ffn_dw_b": nrm((L, 2 * D_FF), 0.02),
        "ffn_w_down": nrm((L, D_FF, D_MODEL), D_FF ** -0.5),
        "norm_final_g": 1.0 + nrm((D_MODEL,), 0.02),
    }


def reference(x_prompt, x_sample, state_shift, state_wkv, state_conv, state_ssm_re, state_ssm_im,
              state_ffn_conv, norm_mix_g, w_in, b_gate, rwkv_mu, rwkv_w0, rwkv_w_up, rwkv_a0,
              rwkv_a_up, rwkv_g_up, rwkv_k_k, rwkv_k_a, rwkv_r_k, rwkv_ln_g, rwkv_ln_b, rwkv_w_out,
              conf_dw_w, conf_dw_b, conf_ln_g, conf_ln_b, conf_w_out, s5_lambda_re, s5_lambda_im,
              s5_log_dt, s5_b_re, s5_b_im, s5_c_re, s5_c_im, s5_d, s5_w_glu, w_out, norm_ffn_g,
              ffn_w_up, ffn_dw_w, ffn_dw_b, ffn_w_down, norm_final_g):
    layers = [dict(
        norm_mix_g=norm_mix_g[l], w_in=w_in[l], b_gate=b_gate[l], rwkv_mu=rwkv_mu[l],
        rwkv_w0=rwkv_w0[l], rwkv_w_up=rwkv_w_up[l], rwkv_a0=rwkv_a0[l], rwkv_a_up=rwkv_a_up[l],
        rwkv_g_up=rwkv_g_up[l], rwkv_k_k=rwkv_k_k[l], rwkv_k_a=rwkv_k_a[l], rwkv_r_k=rwkv_r_k[l],
        rwkv_ln_g=rwkv_ln_g[l], rwkv_ln_b=rwkv_ln_b[l], rwkv_w_out=rwkv_w_out[l],
        conf_dw_w=conf_dw_w[l], conf_dw_b=conf_dw_b[l], conf_ln_g=conf_ln_g[l], conf_ln_b=conf_ln_b[l],
        conf_w_out=conf_w_out[l], s5_lambda_re=s5_lambda_re[l], s5_lambda_im=s5_lambda_im[l],
        s5_log_dt=s5_log_dt[l], s5_b_re=s5_b_re[l], s5_b_im=s5_b_im[l], s5_c_re=s5_c_re[l],
        s5_c_im=s5_c_im[l], s5_d=s5_d[l], s5_w_glu=s5_w_glu[l], w_out=w_out[l],
        norm_ffn_g=norm_ffn_g[l], ffn_w_up=ffn_w_up[l], ffn_dw_w=ffn_dw_w[l], ffn_dw_b=ffn_dw_b[l],
        ffn_w_down=ffn_w_down[l]) for l in range(DEPTH)]
    dt = x_prompt.dtype
    bp = x_prompt.shape[0]
    y_prompt, p_states = run_trunk(
        x_prompt,
        jnp.zeros((DEPTH, bp, N_RWKV_IN), dt),
        jnp.zeros((DEPTH, bp, H_A, HEAD_A, HEAD_A), dt),
        jnp.zeros((DEPTH, bp, CONV_W - 1, C_B), dt),
        jnp.zeros((DEPTH, bp, G_C, P_C), dt),
        jnp.zeros((DEPTH, bp, G_C, P_C), dt),
        jnp.zeros((DEPTH, bp, FFN_CONV_W - 1, 2 * D_FF), dt),
        layers, norm_final_g)
    p_shift, p_wkv, p_conv, p_ssm_re, p_ssm_im, p_ffn_conv = p_states
    y_sample, s_states = run_trunk(
        x_sample, state_shift, state_wkv, state_conv, state_ssm_re, state_ssm_im, state_ffn_conv,
        layers, norm_final_g)
    s_shift, s_wkv, s_conv, s_ssm_re, s_ssm_im, s_ffn_conv = s_states
    return (y_prompt, y_sample, p_shift, p_wkv, p_conv, p_ssm_re, p_ssm_im, p_ffn_conv,
            s_shift, s_wkv, s_conv, s_ssm_re, s_ssm_im, s_ffn_conv)
```

```python
import functools
import math

import jax
import jax.numpy as jnp
from jax import lax
from jax.experimental import pallas as pl
from jax.experimental.pallas import tpu as pltpu

F32 = jnp.float32
BF16 = jnp.bfloat16

VMEM_LIMIT_BYTES = 56 * 1024 * 1024
SUBLANES = 8
LANES = 128

HEAD = 64
LORA_PAIR = 128
RMS_EPS = 1e-6
LN_EPS = 1e-5
GN_EPS = 64e-5
DECAY_SCALE = math.exp(-0.5)


def _params(*sem):
    return pltpu.CompilerParams(dimension_semantics=sem, vmem_limit_bytes=VMEM_LIMIT_BYTES)


def _const_spec(shape):
    zeros = (0,) * len(shape)
    return pl.BlockSpec(shape, lambda *_: zeros)


def _dot(a, b):
    return jnp.dot(a, b, preferred_element_type=F32)


def _head_sum(x, ones_bd):
    hi = x.astype(BF16)
    lo = (x - hi.astype(F32)).astype(BF16)
    return _dot(hi, ones_bd) + _dot(lo, ones_bd)


def _rms_scale(x, g):
    ms = jnp.mean(x * x, axis=-1, keepdims=True)
    return x * lax.rsqrt(ms + RMS_EPS) * g


def _in_proj_kernel(x_ref, g_ref, w_ref, *o_refs, bounds):
    xn = _rms_scale(x_ref[...], g_ref[...]).astype(BF16)
    for o_ref, (lo, hi) in zip(o_refs, bounds):
        o_ref[...] = _dot(xn, w_ref[:, lo:hi])


def _in_proj(x, g, w_bf16, widths, tm):
    rows, d = x.shape
    bounds, lo = [], 0
    for wd in widths:
        bounds.append((lo, lo + wd))
        lo += wd
    return pl.pallas_call(
        functools.partial(_in_proj_kernel, bounds=tuple(bounds)),
        grid=(rows // tm,),
        in_specs=[pl.BlockSpec((tm, d), lambda i: (i, 0)), _const_spec(g.shape),
                  pl.BlockSpec(w_bf16.shape, lambda i: (0, 0), pipeline_mode=pl.Buffered(1))],
        out_specs=[pl.BlockSpec((tm, wd), lambda i: (i, 0)) for wd in widths],
        out_shape=[jax.ShapeDtypeStruct((rows, wd), F32) for wd in widths],
        compiler_params=_params("arbitrary"),
        name="in_proj",
    )(x, g, w_bf16)


def _rwkv_prep_kernel(p_ref, s0_ref, mu_ref, w0_ref, a0_ref, kk_ref, ka_ref, rk_ref, wup_ref, aup_ref,
                      gup_ref, ones_ref, r_o, w_o, k_o, v_o, a_o, b_o, g_o, bonus_o, carry_ref, *, bg, c_a):
    @pl.when(pl.program_id(0) == 0)
    def _():
        carry_ref[...] = s0_ref[...]

    p = p_ref[...]
    tm = p.shape[0]
    prev = carry_ref[...] if tm == bg else jnp.concatenate([carry_ref[...], p[:tm - bg]], axis=0)
    carry_ref[...] = p[tm - bg:]
    xm = p + (prev - p) * mu_ref[...]
    r = xm[:, 0:c_a]
    k = xm[:, c_a:2 * c_a]
    v = xm[:, 2 * c_a:3 * c_a]
    xwa = xm[:, 3 * c_a:3 * c_a + LORA_PAIR]
    xg = xm[:, 3 * c_a + LORA_PAIR:]
    ones_bd = ones_ref[...]
    z = w0_ref[...] + _dot(jnp.tanh(xwa).astype(BF16), wup_ref[...])
    decay = jnp.exp(-DECAY_SCALE * jax.nn.sigmoid(z))
    a = jax.nn.sigmoid(a0_ref[...] + _dot(xwa.astype(BF16), aup_ref[...]))
    g = _dot(jax.nn.sigmoid(xg).astype(BF16), gup_ref[...])
    kk = k * kk_ref[...]
    kk = kk * lax.rsqrt(jnp.maximum(_head_sum(kk * kk, ones_bd), 1e-24))
    k_h = k * (1.0 + (a - 1.0) * ka_ref[...])
    r_o[...] = r
    w_o[...] = decay
    k_o[...] = k_h
    v_o[...] = v
    a_o[...] = -kk
    b_o[...] = kk * a
    g_o[...] = g
    bonus_o[...] = _head_sum(r * k_h * rk_ref[...], ones_bd) * v


def _rwkv_prep(p_r, shift0, prm, bg, tm):
    rows, n_in = p_r.shape
    c_a = prm["w0"].shape[1]
    consts = [prm["mu"], prm["w0"], prm["a0"], prm["k_k"], prm["k_a"], prm["r_k"], prm["w_up"], prm["a_up"],
              prm["g_up"], prm["ones_bd"]]
    return pl.pallas_call(
        functools.partial(_rwkv_prep_kernel, bg=bg, c_a=c_a),
        grid=(rows // tm,),
        in_specs=[pl.BlockSpec((tm, n_in), lambda i: (i, 0)), _const_spec(shift0.shape)]
                 + [_const_spec(c.shape) for c in consts],
        out_specs=[pl.BlockSpec((tm, c_a), lambda i: (i, 0))] * 8,
        out_shape=[jax.ShapeDtypeStruct((rows, c_a), F32)] * 8,
        scratch_shapes=[pltpu.VMEM((bg, n_in), F32)],
        compiler_params=_params("arbitrary"),
        name="rwkv_prep",
    )(p_r, shift0, *consts)


def _wkv_kernel(w_ref, b_ref, k_ref, r_ref, an_ref, v_ref, a0_ref, s0_ref, y_ref, st_ref, sa_ref, *, n_i, tc):
    def row(ref, t, j):
        return ref[t, pl.ds(j, 1), :]

    @pl.when(pl.program_id(1) == 0)
    def _():
        st_ref[...] = s0_ref[...]
        sa0 = jnp.zeros((n_i, LANES), F32)
        for j in range(HEAD):
            sa0 = sa0 + s0_ref[j] * row(a0_ref, 0, j)
        sa_ref[...] = sa0

    def step(t, sa):
        v = v_ref[t]
        y = jnp.zeros((n_i, LANES), F32)
        sa_next = jnp.zeros((n_i, LANES), F32)
        for j in range(HEAD):
            s = st_ref[j] * row(w_ref, t, j) + sa * row(b_ref, t, j) + v * row(k_ref, t, j)
            st_ref[j] = s
            y = y + s * row(r_ref, t, j)
            sa_next = sa_next + s * row(an_ref, t, j)
        y_ref[t] = y
        return sa_next

    sa_ref[...] = lax.fori_loop(0, tc, step, sa_ref[...])


def _wkv(w, b, k, r, a_next, v, a_first, s0, tc):
    t_len, _, n_c = w.shape
    n_i = v.shape[1]
    seq = pl.BlockSpec((tc, HEAD, LANES), lambda c, t: (t, 0, c))
    return pl.pallas_call(
        functools.partial(_wkv_kernel, n_i=n_i, tc=tc),
        grid=(n_c // LANES, t_len // tc),
        in_specs=[seq] * 5 + [pl.BlockSpec((tc, n_i, LANES), lambda c, t: (t, 0, c)),
                              pl.BlockSpec((1, HEAD, LANES), lambda c, t: (0, 0, c)),
                              pl.BlockSpec((HEAD, n_i, LANES), lambda c, t: (0, 0, c))],
        out_specs=[pl.BlockSpec((tc, n_i, LANES), lambda c, t: (t, 0, c)),
                   pl.BlockSpec((HEAD, n_i, LANES), lambda c, t: (0, 0, c))],
        out_shape=[jax.ShapeDtypeStruct((t_len, n_i, n_c), F32), jax.ShapeDtypeStruct((HEAD, n_i, n_c), F32)],
        scratch_shapes=[pltpu.VMEM((n_i, LANES), F32)],
        compiler_params=_params("arbitrary", "arbitrary"),
        name="wkv",
    )(w, b, k, r, a_next, v, a_first, s0)


def _wkv_group(r, w, k, v, a, b, wkv0, bg, n_t):
    c_a = r.shape[1]
    n_h = c_a // HEAD
    n_chain = bg * n_h
    split = LANES // n_chain if n_chain < LANES else 1
    n_i = HEAD // split

    def key_major(x):
        x = x.reshape(n_t, bg, n_h, HEAD).transpose(0, 3, 1, 2).reshape(n_t, HEAD, n_chain)
        return jnp.concatenate([x] * split, axis=-1) if split > 1 else x

    def val_major(x):
        x = x.reshape(n_t, bg, n_h, split, n_i).transpose(0, 4, 3, 1, 2)
        return x.reshape(n_t, n_i, split * n_chain)

    a_km = key_major(a)
    a_next = jnp.concatenate([a_km[1:], jnp.zeros_like(a_km[:1])], axis=0)
    s0 = wkv0.reshape(bg, n_h, split, n_i, HEAD).transpose(4, 3, 2, 0, 1).reshape(HEAD, n_i, split * n_chain)
    tc = min(n_t, 64)
    y, st = _wkv(key_major(w), key_major(b), key_major(k), key_major(r), a_next, val_major(v), a_km[:1], s0, tc)
    y = y.reshape(n_t, n_i, split, bg, n_h).transpose(0, 3, 4, 2, 1).reshape(n_t * bg, c_a)
    st = st.reshape(HEAD, n_i, split, bg, n_h).transpose(3, 4, 2, 1, 0).reshape(bg, n_h, HEAD, HEAD)
    return y, st


def _conf_kernel(p_ref, st0_ref, dww_ref, dwb_ref, lng_ref, lnb_ref, o_ref, st_o_ref, full_ref, *, bg, n_tap,
                 carry):
    tm, c2 = p_ref.shape
    c_b = c2 // 2
    halo = (n_tap - 1) * bg

    @pl.when(pl.program_id(0) == 0)
    def _():
        full_ref[0:halo, :] = st0_ref[...]

    p = p_ref[...]
    full_ref[halo:halo + tm, :] = p[:, :c_b] * jax.nn.sigmoid(p[:, c_b:])
    acc = jnp.zeros((tm, c_b), F32) + dwb_ref[...]
    for tap in range(n_tap):
        acc = acc + dww_ref[pl.ds(tap, 1), :] * full_ref[tap * bg:tap * bg + tm, :]
    mean = jnp.mean(acc, axis=-1, keepdims=True)
    cen = acc - mean
    var = jnp.mean(cen * cen, axis=-1, keepdims=True)
    n = cen * lax.rsqrt(var + LN_EPS) * lng_ref[...] + lnb_ref[...]
    o_ref[...] = (n * jax.nn.sigmoid(n)).astype(o_ref.dtype)
    tail = full_ref[tm:tm + halo, :]
    st_o_ref[...] = tail
    if carry:
        full_ref[0:halo, :] = tail


def _conformer(p_c, st0, prm, bg, tm):
    rows, c2 = p_c.shape
    c_b = c2 // 2
    n_tap = prm["dw_w"].shape[0]
    halo = (n_tap - 1) * bg
    consts = [prm["dw_w"], prm["dw_b"], prm["ln_g"], prm["ln_b"]]
    return pl.pallas_call(
        functools.partial(_conf_kernel, bg=bg, n_tap=n_tap, carry=rows > tm),
        grid=(rows // tm,),
        in_specs=[pl.BlockSpec((tm, c2), lambda i: (i, 0)), _const_spec(st0.shape)]
                 + [_const_spec(c.shape) for c in consts],
        out_specs=[pl.BlockSpec((tm, c_b), lambda i: (i, 0)), _const_spec((halo, c_b))],
        out_shape=[jax.ShapeDtypeStruct((rows, c_b), BF16), jax.ShapeDtypeStruct((halo, c_b), F32)],
        scratch_shapes=[pltpu.VMEM((halo + tm, c_b), F32)],
        compiler_params=_params("arbitrary"),
        name="conformer",
    )(p_c, st0, *consts)


S5_LANE_CHUNK = 512


def _s5_kernel(u_ref, h0r_ref, h0i_ref, lamr_ref, lami_ref, dt_ref, bre_ref, bim_ref, cre_ref, cim_ref, d_ref,
               wglu_ref, o_ref, hr_o, hi_o, sre_ref, sim_ref, coef_ref, hr_s, hi_s, *, bg):
    tm = u_ref.shape[0]
    n_state = sre_ref.shape[1]
    d_model = o_ref.shape[1]

    @pl.when(pl.program_id(0) == 0)
    def _():
        lr = lamr_ref[...]
        li = lami_ref[...]
        dt = jnp.exp(dt_ref[...])
        mag = jnp.exp(lr * dt)
        ab_re = mag * jnp.cos(li * dt)
        ab_im = mag * jnp.sin(li * dt)
        den = lr * lr + li * li
        coef_ref[0:1, :] = ab_re
        coef_ref[1:2, :] = ab_im
        coef_ref[2:3, :] = ((ab_re - 1.0) * lr + ab_im * li) / den
        coef_ref[3:4, :] = (ab_im * lr - (ab_re - 1.0) * li) / den
        hr_s[...] = h0r_ref[...]
        hi_s[...] = h0i_ref[...]

    u = u_ref[...]
    ub = u.astype(BF16)
    bu_re = _dot(ub, bre_ref[...])
    bu_im = _dot(ub, bim_ref[...])
    q_re = coef_ref[2:3, :]
    q_im = coef_ref[3:4, :]
    sre_ref[...] = q_re * bu_re - q_im * bu_im
    sim_ref[...] = q_re * bu_im + q_im * bu_re

    n_step = tm // bg
    for lc in range(n_state // S5_LANE_CHUNK):
        lanes = slice(lc * S5_LANE_CHUNK, (lc + 1) * S5_LANE_CHUNK)
        ab_re = jnp.broadcast_to(coef_ref[0:1, lanes], (SUBLANES, S5_LANE_CHUNK))
        ab_im = jnp.broadcast_to(coef_ref[1:2, lanes], (SUBLANES, S5_LANE_CHUNK))

        def strip(rb, _):
            r0 = pl.multiple_of(rb * SUBLANES, SUBLANES)

            def step(t, carry):
                s_re, s_im = carry
                rows = pl.ds(pl.multiple_of(t * bg + r0, SUBLANES), SUBLANES)
                n_re = ab_re * s_re - ab_im * s_im + sre_ref[rows, lanes]
                n_im = ab_re * s_im + ab_im * s_re + sim_ref[rows, lanes]
                sre_ref[rows, lanes] = n_re
                sim_ref[rows, lanes] = n_im
                return n_re, n_im

            rs = pl.ds(r0, SUBLANES)
            s_re, s_im = lax.fori_loop(0, n_step, step, (hr_s[rs, lanes], hi_s[rs, lanes]))
            hr_s[rs, lanes] = s_re
            hi_s[rs, lanes] = s_im
            return 0

        lax.fori_loop(0, bg // SUBLANES, strip, 0)

    hr_o[...] = hr_s[...]
    hi_o[...] = hi_s[...]
    y = _dot(sre_ref[...].astype(BF16), cre_ref[...]) - _dot(sim_ref[...].astype(BF16), cim_ref[...])
    y = y + d_ref[...] * u
    z = _dot(y.astype(BF16), wglu_ref[...])
    o_ref[...] = z[:, :d_model] * jax.nn.sigmoid(z[:, d_model:])


def _s5(u, h0r, h0i, prm, bg, tm):
    rows, c_c = u.shape
    n_state = h0r.shape[1]
    d_model = prm["w_glu"].shape[1] // 2
    consts = [prm["lam_re"], prm["lam_im"], prm["log_dt"], prm["b_re"], prm["b_im"], prm["c_re"], prm["c_im"],
              prm["d"], prm["w_glu"]]
    return pl.pallas_call(
        functools.partial(_s5_kernel, bg=bg),
        grid=(rows // tm,),
        in_specs=[pl.BlockSpec((tm, c_c), lambda i: (i, 0)), _const_spec(h0r.shape), _const_spec(h0i.shape)]
                 + [_const_spec(c.shape) for c in consts],
        out_specs=[pl.BlockSpec((tm, d_model), lambda i: (i, 0)), _const_spec(h0r.shape), _const_spec(h0i.shape)],
        out_shape=[jax.ShapeDtypeStruct((rows, d_model), F32), jax.ShapeDtypeStruct(h0r.shape, F32),
                   jax.ShapeDtypeStruct(h0i.shape, F32)],
        scratch_shapes=[pltpu.VMEM((tm, n_state), F32), pltpu.VMEM((tm, n_state), F32),
                        pltpu.VMEM((SUBLANES, n_state), F32), pltpu.VMEM(h0r.shape, F32),
                        pltpu.VMEM(h0i.shape, F32)],
        compiler_params=_params("arbitrary"),
        name="s5",
    )(u, h0r, h0i, *consts)


def _merge_kernel(y_ref, bonus_ref, g_ref, yb_ref, yc_ref, pg_ref, x_ref, lng_ref, lnb_ref, ones_ref, woa_ref,
                  wob_ref, bgate_ref, wout_ref, o_ref):
    d_model = x_ref.shape[1]
    ones_bd = ones_ref[...]
    y = y_ref[...]
    cen = y - _head_sum(y, ones_bd) * (1.0 / HEAD)
    var = _head_sum(cen * cen, ones_bd) * (1.0 / HEAD)
    yn = cen * lax.rsqrt(var + GN_EPS) * lng_ref[...] + lnb_ref[...]
    y_a = _dot(((yn + bonus_ref[...]) * g_ref[...]).astype(BF16), woa_ref[...])
    y_b = _dot(yb_ref[...], wob_ref[...])
    gates = jax.nn.sigmoid(pg_ref[...] + bgate_ref[...])
    h = (gates[:, :d_model] * y_a + gates[:, d_model:2 * d_model] * y_b + gates[:, 2 * d_model:] * yc_ref[...])
    o_ref[...] = x_ref[...] + _dot(h.astype(BF16), wout_ref[...])


def _merge(y, bonus, g, yb, yc, pg, x, prm, tm):
    rows, d_model = x.shape
    acts = [y, bonus, g, yb, yc, pg, x]
    consts = [prm["ln_g"], prm["ln_b"], prm["ones_bd"], prm["rwkv_w_out"], prm["conf_w_out"], prm["b_gate"],
              prm["w_out"]]
    return pl.pallas_call(
        _merge_kernel,
        grid=(rows // tm,),
        in_specs=[pl.BlockSpec((tm, a.shape[1]), lambda i: (i, 0)) for a in acts]
                 + [_const_spec(c.shape) for c in consts],
        out_specs=pl.BlockSpec((tm, d_model), lambda i: (i, 0)),
        out_shape=jax.ShapeDtypeStruct((rows, d_model), F32),
        compiler_params=_params("arbitrary"),
        name="merge",
    )(*acts, *consts)


def _ffn_kernel(x_ref, st1_ref, st2_ref, g_ref, wu1_ref, wu2_ref, dw1_ref, dw2_ref, db1_ref, db2_ref, wd_ref,
                gf_ref, o_ref, carry_ref, xn_ref, acc_ref, full1_ref, full2_ref, *, bg, n_tap, final_norm):
    i = pl.program_id(0)
    f = pl.program_id(1)
    tm = x_ref.shape[0]
    halo = (n_tap - 1) * bg

    @pl.when(f == 0)
    def _():
        xn_ref[...] = _rms_scale(x_ref[...], g_ref[...]).astype(BF16)
        acc_ref[...] = jnp.zeros_like(acc_ref)

    @pl.when(i == 0)
    def _():
        carry_ref[f, 0] = st1_ref[...]
        carry_ref[f, 1] = st2_ref[...]

    xn = xn_ref[...]
    halves = []
    for h, (wu_ref, dw_ref, db_ref, full_ref) in enumerate(
            ((wu1_ref, dw1_ref, db1_ref, full1_ref), (wu2_ref, dw2_ref, db2_ref, full2_ref))):
        full_ref[0:halo, :] = carry_ref[f, h]
        full_ref[halo:halo + tm, :] = _dot(xn, wu_ref[...])
        c = jnp.zeros((tm, full_ref.shape[1]), F32) + db_ref[...]
        for tap in range(n_tap):
            c = c + dw_ref[pl.ds(tap, 1), :] * full_ref[tap * bg:tap * bg + tm, :]
        carry_ref[f, h] = full_ref[tm:tm + halo, :]
        halves.append(c)
    act = (halves[0] * jax.nn.sigmoid(halves[0]) * halves[1]).astype(BF16)
    acc_ref[...] += _dot(act, wd_ref[...])

    @pl.when(f == pl.num_programs(1) - 1)
    def _():
        out = x_ref[...] + acc_ref[...]
        o_ref[...] = _rms_scale(out, gf_ref[...]) if final_norm else out


def _conv_ffn(x, st0, prm, final_g, bg, tm, final_norm):
    rows, d_model = x.shape
    d_ff = prm["w_down"].shape[0]
    n_tap = prm["dw_w"].shape[0]
    halo = (n_tap - 1) * bg
    n_f = 2
    fc = d_ff // n_f
    assert fc * n_f == d_ff and fc % LANES == 0
    lo = lambda i, f: (0, f)
    hi = lambda i, f: (0, f + n_f)
    out, carry = pl.pallas_call(
        functools.partial(_ffn_kernel, bg=bg, n_tap=n_tap, final_norm=final_norm),
        grid=(rows // tm, n_f),
        in_specs=[pl.BlockSpec((tm, d_model), lambda i, f: (i, 0)),
                  pl.BlockSpec((halo, fc), lo), pl.BlockSpec((halo, fc), hi),
                  _const_spec(prm["norm_g"].shape),
                  pl.BlockSpec((d_model, fc), lo), pl.BlockSpec((d_model, fc), hi),
                  pl.BlockSpec((n_tap, fc), lo), pl.BlockSpec((n_tap, fc), hi),
                  pl.BlockSpec((1, fc), lo), pl.BlockSpec((1, fc), hi),
                  pl.BlockSpec((fc, d_model), lambda i, f: (f, 0)),
                  _const_spec(final_g.shape)],
        out_specs=[pl.BlockSpec((tm, d_model), lambda i, f: (i, 0)), _const_spec((n_f, 2, halo, fc))],
        out_shape=[jax.ShapeDtypeStruct((rows, d_model), F32), jax.ShapeDtypeStruct((n_f, 2, halo, fc), F32)],
        scratch_shapes=[pltpu.VMEM((tm, d_model), BF16), pltpu.VMEM((tm, d_model), F32),
                        pltpu.VMEM((halo + tm, fc), F32), pltpu.VMEM((halo + tm, fc), F32)],
        compiler_params=_params("arbitrary", "arbitrary"),
        name="conv_ffn",
    )(x, st0, st0, prm["norm_g"], prm["w_up"], prm["w_up"], prm["dw_w"], prm["dw_w"], prm["dw_b"], prm["dw_b"],
      prm["w_down"], final_g)
    return out, carry.transpose(2, 1, 0, 3).reshape(halo, 2 * d_ff)


def _tile(rows, bg, pref):
    tm = max(bg, min(pref, rows))
    while rows % tm or tm % bg:
        tm -= bg
    return tm


def _time_major(st):
    return st.transpose(1, 0, 2).reshape(-1, st.shape[-1])


def _seq_major(st, bg):
    return st.reshape(-1, bg, st.shape[-1]).transpose(1, 0, 2)


def _run_trunk(x_bm, states, layers, final_g):
    bg, n_t, d_model = x_bm.shape
    rows = bg * n_t
    x = x_bm.transpose(1, 0, 2).reshape(rows, d_model)
    shift, wkv, conv, ssm_re, ssm_im, ffn_conv = states
    outs = ([], [], [], [], [], [])
    for l, prm in enumerate(layers):
        p_r, p_c, p_s, p_g = _in_proj(x, prm["norm_mix_g"], prm["w_in"], prm["in_widths"], _tile(rows, bg, 256))
        r, w, k, v, a, b, g, bonus = _rwkv_prep(p_r, shift[l], prm["rwkv"], bg, _tile(rows, bg, 512))
        y, wkv_n = _wkv_group(r, w, k, v, a, b, wkv[l], bg, n_t)
        yb, conv_n = _conformer(p_c, _time_major(conv[l]), prm["conf"], bg, _tile(rows, bg, 1024 if bg > 8 else 512))
        n_state = ssm_re.shape[2] * ssm_re.shape[3]
        yc, re_n, im_n = _s5(p_s, ssm_re[l].reshape(bg, n_state), ssm_im[l].reshape(bg, n_state), prm["s5"], bg,
                             _tile(rows, bg, 256 if bg > 8 else 512))
        x = _merge(y, bonus, g, yb, yc, p_g, x, prm["merge"], _tile(rows, bg, 512))
        x, ffn_n = _conv_ffn(x, _time_major(ffn_conv[l]), prm["ffn"], final_g, bg, _tile(rows, bg, 512),
                             final_norm=(l == len(layers) - 1))
        new = (p_r[rows - bg:], wkv_n, _seq_major(conv_n, bg), re_n.reshape(ssm_re.shape[1:]),
               im_n.reshape(ssm_im.shape[1:]), _seq_major(ffn_n, bg))
        for lst, s in zip(outs, new):
            lst.append(s)
    y = x.reshape(n_t, bg, d_model).transpose(1, 0, 2)
    return y, [jnp.stack(lst) for lst in outs]


def _block_diag(blocks):
    n_g, a, b = blocks.shape
    eye = jnp.eye(n_g, dtype=blocks.dtype)
    return (eye[:, None, :, None] * blocks[:, :, None, :]).reshape(n_g * a, n_g * b)


def _prep_layer(l, w):
    row = lambda v: v.reshape(1, -1).astype(F32)
    c_a = w["rwkv_w0"].shape[1]
    n_rwkv = w["rwkv_mu"].shape[1]
    c_b = w["conf_dw_b"].shape[1]
    c_c = w["s5_d"].shape[1]
    d_model = w["w_out"].shape[1]
    lora_w = w["rwkv_w_up"].shape[1]
    lora_a = w["rwkv_a_up"].shape[1]
    assert lora_w + lora_a == LORA_PAIR and n_rwkv == 3 * c_a + LORA_PAIR + w["rwkv_g_up"].shape[1]
    n_grp, p_c = w["s5_lambda_re"].shape[1:]
    ones_bd = _block_diag(jnp.ones((c_a // HEAD, HEAD, HEAD), F32)).astype(BF16)
    rwkv = dict(
        mu=row(w["rwkv_mu"][l]), w0=row(w["rwkv_w0"][l]), a0=row(w["rwkv_a0"][l]), k_k=row(w["rwkv_k_k"][l]),
        k_a=row(w["rwkv_k_a"][l]), r_k=row(w["rwkv_r_k"][l]),
        w_up=jnp.concatenate([w["rwkv_w_up"][l], jnp.zeros((lora_a, c_a), F32)], axis=0).astype(BF16),
        a_up=jnp.concatenate([jnp.zeros((lora_w, c_a), F32), w["rwkv_a_up"][l]], axis=0).astype(BF16),
        g_up=w["rwkv_g_up"][l].astype(BF16), ones_bd=ones_bd)
    conf = dict(dw_w=w["conf_dw_w"][l], dw_b=row(w["conf_dw_b"][l]), ln_g=row(w["conf_ln_g"][l]),
                ln_b=row(w["conf_ln_b"][l]))
    s5 = dict(
        lam_re=row(w["s5_lambda_re"][l]), lam_im=row(w["s5_lambda_im"][l]),
        log_dt=row(jnp.repeat(w["s5_log_dt"][l], p_c)),
        b_re=_block_diag(w["s5_b_re"][l].transpose(0, 2, 1)).astype(BF16),
        b_im=_block_diag(w["s5_b_im"][l].transpose(0, 2, 1)).astype(BF16),
        c_re=_block_diag(w["s5_c_re"][l].transpose(0, 2, 1)).astype(BF16),
        c_im=_block_diag(w["s5_c_im"][l].transpose(0, 2, 1)).astype(BF16),
        d=row(w["s5_d"][l]), w_glu=w["s5_w_glu"][l].astype(BF16))
    merge = dict(ln_g=row(w["rwkv_ln_g"][l]), ln_b=row(w["rwkv_ln_b"][l]), ones_bd=ones_bd,
                 rwkv_w_out=w["rwkv_w_out"][l].astype(BF16), conf_w_out=w["conf_w_out"][l].astype(BF16),
                 b_gate=row(w["b_gate"][l]), w_out=w["w_out"][l].astype(BF16))
    ffn = dict(norm_g=row(w["norm_ffn_g"][l]), w_up=w["ffn_w_up"][l].astype(BF16), dw_w=w["ffn_dw_w"][l],
               dw_b=row(w["ffn_dw_b"][l]), w_down=w["ffn_w_down"][l].astype(BF16))
    in_widths = (n_rwkv, 2 * c_b, c_c, w["b_gate"].shape[1])
    assert sum(in_widths) == w["w_in"].shape[2] and d_model * 3 == in_widths[3] and n_grp * p_c > 0
    return dict(norm_mix_g=row(w["norm_mix_g"][l]), w_in=w["w_in"][l].astype(BF16), in_widths=in_widths,
                rwkv=rwkv, conf=conf, s5=s5, merge=merge, ffn=ffn)


def kernel(x_prompt, x_sample, state_shift, state_wkv, state_conv, state_ssm_re, state_ssm_im, state_ffn_conv, norm_mix_g, w_in, b_gate, rwkv_mu, rwkv_w0, rwkv_w_up, rwkv_a0, rwkv_a_up, rwkv_g_up, rwkv_k_k, rwkv_k_a, rwkv_r_k, rwkv_ln_g, rwkv_ln_b, rwkv_w_out, conf_dw_w, conf_dw_b, conf_ln_g, conf_ln_b, conf_w_out, s5_lambda_re, s5_lambda_im, s5_log_dt, s5_b_re, s5_b_im, s5_c_re, s5_c_im, s5_d, s5_w_glu, w_out, norm_ffn_g, ffn_w_up, ffn_dw_w, ffn_dw_b, ffn_w_down, norm_final_g):
    w = dict(norm_mix_g=norm_mix_g, w_in=w_in, b_gate=b_gate, rwkv_mu=rwkv_mu, rwkv_w0=rwkv_w0,
             rwkv_w_up=rwkv_w_up, rwkv_a0=rwkv_a0, rwkv_a_up=rwkv_a_up, rwkv_g_up=rwkv_g_up, rwkv_k_k=rwkv_k_k,
             rwkv_k_a=rwkv_k_a, rwkv_r_k=rwkv_r_k, rwkv_ln_g=rwkv_ln_g, rwkv_ln_b=rwkv_ln_b,
             rwkv_w_out=rwkv_w_out, conf_dw_w=conf_dw_w, conf_dw_b=conf_dw_b, conf_ln_g=conf_ln_g,
             conf_ln_b=conf_ln_b, conf_w_out=conf_w_out, s5_lambda_re=s5_lambda_re, s5_lambda_im=s5_lambda_im,
             s5_log_dt=s5_log_dt, s5_b_re=s5_b_re, s5_b_im=s5_b_im, s5_c_re=s5_c_re, s5_c_im=s5_c_im, s5_d=s5_d,
             s5_w_glu=s5_w_glu, w_out=w_out, norm_ffn_g=norm_ffn_g, ffn_w_up=ffn_w_up, ffn_dw_w=ffn_dw_w,
             ffn_dw_b=ffn_dw_b, ffn_w_down=ffn_w_down)
    depth = w_in.shape[0]
    layers = [_prep_layer(l, w) for l in range(depth)]
    final_g = norm_final_g.reshape(1, -1)
    bp = x_prompt.shape[0]
    zeros_like_state = lambda s: jnp.zeros((depth, bp) + s.shape[2:], s.dtype)
    prompt_states = tuple(zeros_like_state(s) for s in
                          (state_shift, state_wkv, state_conv, state_ssm_re, state_ssm_im, state_ffn_conv))
    y_prompt, p_states = _run_trunk(x_prompt, prompt_states, layers, final_g)
    y_sample, s_states = _run_trunk(
        x_sample, (state_shift, state_wkv, state_conv, state_ssm_re, state_ssm_im, state_ffn_conv), layers, final_g)
    return (y_prompt, y_sample, *p_states, *s_states)
```

```python
import functools
import math

import jax
import jax.numpy as jnp
from jax import lax
from jax.experimental import pallas as pl
from jax.experimental.pallas import tpu as pltpu

F32 = jnp.float32
BF16 = jnp.bfloat16

VMEM_LIMIT_BYTES = 56 * 1024 * 1024
SUBLANES = 8
LANES = 128

HEAD = 64
LORA_PAIR = 128
RMS_EPS = 1e-6
LN_EPS = 1e-5
GN_EPS = 64e-5
DECAY_SCALE = math.exp(-0.5)


def _params(*sem):
    return pltpu.CompilerParams(dimension_semantics=sem, vmem_limit_bytes=VMEM_LIMIT_BYTES)


def _const_spec(shape):
    zeros = (0,) * len(shape)
    return pl.BlockSpec(shape, lambda *_: zeros)


def _dot(a, b):
    return jnp.dot(a, b, preferred_element_type=F32)


def _head_sum(x, ones_bd):
    hi = x.astype(BF16)
    lo = (x - hi.astype(F32)).astype(BF16)
    return _dot(hi, ones_bd) + _dot(lo, ones_bd)


def _rms_scale(x, g):
    ms = jnp.mean(x * x, axis=-1, keepdims=True)
    return x * lax.rsqrt(ms + RMS_EPS) * g


def _in_proj_kernel(x_ref, g_ref, w_ref, *o_refs, bounds):
    xn = _rms_scale(x_ref[...], g_ref[...]).astype(BF16)
    for o_ref, (lo, hi) in zip(o_refs, bounds):
        o_ref[...] = _dot(xn, w_ref[:, lo:hi])


def _in_proj(x, g, w_bf16, widths, tm):
    rows, d = x.shape
    bounds, lo = [], 0
    for wd in widths:
        bounds.append((lo, lo + wd))
        lo += wd
    return pl.pallas_call(
        functools.partial(_in_proj_kernel, bounds=tuple(bounds)),
        grid=(rows // tm,),
        in_specs=[pl.BlockSpec((tm, d), lambda i: (i, 0)), _const_spec(g.shape),
                  pl.BlockSpec(w_bf16.shape, lambda i: (0, 0), pipeline_mode=pl.Buffered(1))],
        out_specs=[pl.BlockSpec((tm, wd), lambda i: (i, 0)) for wd in widths],
        out_shape=[jax.ShapeDtypeStruct((rows, wd), F32) for wd in widths],
        compiler_params=_params("arbitrary"),
        name="in_proj",
    )(x, g, w_bf16)


def _rwkv_prep_kernel(p_ref, s0_ref, mu_ref, w0_ref, a0_ref, kk_ref, ka_ref, rk_ref, wup_ref, aup_ref,
                      gup_ref, ones_ref, r_o, w_o, k_o, v_o, a_o, b_o, g_o, bonus_o, carry_ref, *, bg, c_a):
    @pl.when(pl.program_id(0) == 0)
    def _():
        carry_ref[...] = s0_ref[...]

    p = p_ref[...]
    tm = p.shape[0]
    prev = carry_ref[...] if tm == bg else jnp.concatenate([carry_ref[...], p[:tm - bg]], axis=0)
    carry_ref[...] = p[tm - bg:]
    xm = p + (prev - p) * mu_ref[...]
    r = xm[:, 0:c_a]
    k = xm[:, c_a:2 * c_a]
    v = xm[:, 2 * c_a:3 * c_a]
    xwa = xm[:, 3 * c_a:3 * c_a + LORA_PAIR]
    xg = xm[:, 3 * c_a + LORA_PAIR:]
    ones_bd = ones_ref[...]
    z = w0_ref[...] + _dot(jnp.tanh(xwa).astype(BF16), wup_ref[...])
    decay = jnp.exp(-DECAY_SCALE * jax.nn.sigmoid(z))
    a = jax.nn.sigmoid(a0_ref[...] + _dot(xwa.astype(BF16), aup_ref[...]))
    g = _dot(jax.nn.sigmoid(xg).astype(BF16), gup_ref[...])
    kk = k * kk_ref[...]
    kk = kk * lax.rsqrt(jnp.maximum(_head_sum(kk * kk, ones_bd), 1e-24))
    k_h = k * (1.0 + (a - 1.0) * ka_ref[...])
    r_o[...] = r
    w_o[...] = decay
    k_o[...] = k_h
    v_o[...] = v
    a_o[...] = -kk
    b_o[...] = kk * a
    g_o[...] = g
    bonus_o[...] = _head_sum(r * k_h * rk_ref[...], ones_bd) * v


def _rwkv_prep(p_r, shift0, prm, bg, tm):
    rows, n_in = p_r.shape
    c_a = prm["w0"].shape[1]
    consts = [prm["mu"], prm["w0"], prm["a0"], prm["k_k"], prm["k_a"], prm["r_k"], prm["w_up"], prm["a_up"],
              prm["g_up"], prm["ones_bd"]]
    return pl.pallas_call(
        functools.partial(_rwkv_prep_kernel, bg=bg, c_a=c_a),
        grid=(rows // tm,),
        in_specs=[pl.BlockSpec((tm, n_in), lambda i: (i, 0)), _const_spec(shift0.shape)]
                 + [_const_spec(c.shape) for c in consts],
        out_specs=[pl.BlockSpec((tm, c_a), lambda i: (i, 0))] * 8,
        out_shape=[jax.ShapeDtypeStruct((rows, c_a), F32)] * 8,
        scratch_shapes=[pltpu.VMEM((bg, n_in), F32)],
        compiler_params=_params("arbitrary"),
        name="rwkv_prep",
    )(p_r, shift0, *consts)


def _wkv_kernel(r_ref, w_ref, k_ref, v_ref, a_ref, b_ref, s0_ref, y_ref, st_ref, *scratch, n_i, tc, nb, n_h, n_chunk):
    slots = (scratch[0:6], scratch[6:12])
    zy = scratch[12]
    dup = HEAD // n_i
    n_pair = tc // 2
    g = pl.program_id(1)
    low = lax.broadcasted_iota(jnp.int32, (nb, LANES), 1) < HEAD
    low_i = lax.broadcasted_iota(jnp.int32, (n_i, LANES), 1) < HEAD
    swap = lambda x: pltpu.roll(x, HEAD, 1)

    def to_chain_minor(p, slot):
        t0 = 2 * p
        for src_ref, dst_ref in zip((r_ref, w_ref, k_ref, v_ref, a_ref, b_ref), slots[slot]):
            pieces = []
            for q in range(n_h // 2):
                x0 = src_ref[t0, :, q * LANES:(q + 1) * LANES]
                x1 = src_ref[t0 + 1, :, q * LANES:(q + 1) * LANES]
                pieces.append(jnp.where(low, x0, swap(x1)))
                pieces.append(jnp.where(low, swap(x0), x1))
            tt = jnp.concatenate(pieces * dup, axis=0).T
            dst_ref[t0] = tt[:HEAD]
            dst_ref[t0 + 1] = tt[HEAD:]

    def row(ref, t, j):
        return ref[t, pl.ds(j, 1), :]

    def step(t, sa, slot):
        zr, zw, zk, zv, za, zb = slots[slot]
        t_next = jnp.minimum(t + 1, tc - 1)
        v = zv[t] if dup == 1 else jnp.where(low_i, zv[t, :n_i, :], zv[t, n_i:, :])
        y = jnp.zeros((n_i, LANES), F32)
        sa_next = jnp.zeros((n_i, LANES), F32)
        for j in range(HEAD):
            s = st_ref[j] * row(zw, t, j) + sa * row(zb, t, j) + v * row(zk, t, j)
            st_ref[j] = s
            y = y + s * row(zr, t, j)
            sa_next = sa_next + s * row(za, t_next, j)
        zy[t] = y
        return sa_next

    def y_pair(p):
        t0 = 2 * p
        ys = []
        for t in (t0, t0 + 1):
            y = zy[t]
            ys += [y] if dup == 1 else [y, swap(y)]
        wt = jnp.concatenate(ys, axis=0).T
        for q in range(n_h // 2):
            even = wt[2 * q * nb:(2 * q + 1) * nb]
            odd = wt[(2 * q + 1) * nb:(2 * q + 2) * nb]
            y_ref[t0, :, q * LANES:(q + 1) * LANES] = jnp.where(low, even, swap(odd))
            y_ref[t0 + 1, :, q * LANES:(q + 1) * LANES] = jnp.where(low, swap(even), odd)

    def recurrence(slot, with_relayout):
        za = slots[slot][4]
        sa0 = jnp.zeros((n_i, LANES), F32)
        for j in range(HEAD):
            sa0 = sa0 + st_ref[j] * row(za, 0, j)

        def body(p, sa):
            sa = step(2 * p, sa, slot)
            if with_relayout:
                to_chain_minor(p, 1 - slot)
            y_pair(jnp.maximum(p - 1, 0))
            return step(2 * p + 1, sa, slot)

        lax.fori_loop(0, n_pair, body, sa0)
        y_pair(n_pair - 1)

    @pl.when(g == 0)
    def _():
        st_ref[...] = s0_ref[...]
        zy[...] = jnp.zeros_like(zy)

        def first(p, _):
            to_chain_minor(p, 0)
            return 0
        lax.fori_loop(0, n_pair, first, 0, unroll=2)

    for parity in range(2 if n_chunk > 1 else 0):
        @pl.when(jnp.logical_and(jnp.logical_and(g > 0, g < n_chunk), g % 2 == parity))
        def _():
            recurrence(1 - parity, True)

    @pl.when(g == n_chunk)
    def _():
        recurrence((n_chunk - 1) % 2, False)


def _wkv_group(r, w, k, v, a, b, wkv0, bg, n_t):
    c_a = r.shape[1]
    n_h = c_a // HEAD
    dup = max(1, LANES // (bg * n_h))
    nb = LANES // (n_h * dup)
    n_i = HEAD // dup
    n_blk = bg // nb
    tc = min(n_t, 64)
    assert n_h % 2 == 0 and nb * n_blk == bg and nb % SUBLANES == 0 and tc % 2 == 0 and n_t % tc == 0
    s0 = wkv0.reshape(n_blk, nb, n_h, dup, n_i, HEAD).transpose(5, 4, 0, 3, 2, 1).reshape(HEAD, n_i, n_blk * LANES)
    n_chunk = n_t // tc
    tok_in = pl.BlockSpec((tc, nb, c_a), lambda c, g: (jnp.minimum(g, n_chunk - 1), c, 0))
    tok_out = pl.BlockSpec((tc, nb, c_a), lambda c, g: (jnp.maximum(g - 1, 0), c, 0))
    state = pl.BlockSpec((HEAD, n_i, LANES), lambda c, g: (0, 0, c))
    y, st = pl.pallas_call(
        functools.partial(_wkv_kernel, n_i=n_i, tc=tc, nb=nb, n_h=n_h, n_chunk=n_chunk),
        grid=(n_blk, n_chunk + 1),
        in_specs=[tok_in] * 6 + [state],
        out_specs=[tok_out, state],
        out_shape=[jax.ShapeDtypeStruct((n_t, bg, c_a), F32), jax.ShapeDtypeStruct(s0.shape, F32)],
        scratch_shapes=[pltpu.VMEM((tc, HEAD, LANES), F32)] * 12 + [pltpu.VMEM((tc, n_i, LANES), F32)],
        compiler_params=_params("arbitrary", "arbitrary"),
        name="wkv",
    )(*(x.reshape(n_t, bg, c_a) for x in (r, w, k, v, a, b)), s0)
    st = st.reshape(HEAD, n_i, n_blk, dup, n_h, nb).transpose(2, 5, 4, 3, 1, 0).reshape(bg, n_h, HEAD, HEAD)
    return y.reshape(n_t * bg, c_a), st


def _conf_kernel(p_ref, st0_ref, dww_ref, dwb_ref, lng_ref, lnb_ref, o_ref, st_o_ref, full_ref, *, bg, n_tap,
                 carry):
    tm, c2 = p_ref.shape
    c_b = c2 // 2
    halo = (n_tap - 1) * bg

    @pl.when(pl.program_id(0) == 0)
    def _():
        full_ref[0:halo, :] = st0_ref[...]

    p = p_ref[...]
    full_ref[halo:halo + tm, :] = p[:, :c_b] * jax.nn.sigmoid(p[:, c_b:])
    acc = jnp.zeros((tm, c_b), F32) + dwb_ref[...]
    for tap in range(n_tap):
        acc = acc + dww_ref[pl.ds(tap, 1), :] * full_ref[tap * bg:tap * bg + tm, :]
    mean = jnp.mean(acc, axis=-1, keepdims=True)
    cen = acc - mean
    var = jnp.mean(cen * cen, axis=-1, keepdims=True)
    n = cen * lax.rsqrt(var + LN_EPS) * lng_ref[...] + lnb_ref[...]
    o_ref[...] = (n * jax.nn.sigmoid(n)).astype(o_ref.dtype)
    tail = full_ref[tm:tm + halo, :]
    st_o_ref[...] = tail
    if carry:
        full_ref[0:halo, :] = tail


def _conformer(p_c, st0, prm, bg, tm):
    rows, c2 = p_c.shape
    c_b = c2 // 2
    n_tap = prm["dw_w"].shape[0]
    halo = (n_tap - 1) * bg
    consts = [prm["dw_w"], prm["dw_b"], prm["ln_g"], prm["ln_b"]]
    return pl.pallas_call(
        functools.partial(_conf_kernel, bg=bg, n_tap=n_tap, carry=rows > tm),
        grid=(rows // tm,),
        in_specs=[pl.BlockSpec((tm, c2), lambda i: (i, 0)), _const_spec(st0.shape)]
                 + [_const_spec(c.shape) for c in consts],
        out_specs=[pl.BlockSpec((tm, c_b), lambda i: (i, 0)), _const_spec((halo, c_b))],
        out_shape=[jax.ShapeDtypeStruct((rows, c_b), BF16), jax.ShapeDtypeStruct((halo, c_b), F32)],
        scratch_shapes=[pltpu.VMEM((halo + tm, c_b), F32)],
        compiler_params=_params("arbitrary"),
        name="conformer",
    )(p_c, st0, *consts)


S5_LANE_CHUNK = 512


def _s5_kernel(u_ref, h0r_ref, h0i_ref, lamr_ref, lami_ref, dt_ref, bre_ref, bim_ref, cre_ref, cim_ref, d_ref,
               wglu_ref, o_ref, hr_o, hi_o, sre_ref, sim_ref, coef_ref, hr_s, hi_s, *, bg):
    tm = u_ref.shape[0]
    n_state = sre_ref.shape[1]
    d_model = o_ref.shape[1]

    @pl.when(pl.program_id(0) == 0)
    def _():
        lr = lamr_ref[...]
        li = lami_ref[...]
        dt = jnp.exp(dt_ref[...])
        mag = jnp.exp(lr * dt)
        ab_re = mag * jnp.cos(li * dt)
        ab_im = mag * jnp.sin(li * dt)
        den = lr * lr + li * li
        coef_ref[0:1, :] = ab_re
        coef_ref[1:2, :] = ab_im
        coef_ref[2:3, :] = ((ab_re - 1.0) * lr + ab_im * li) / den
        coef_ref[3:4, :] = (ab_im * lr - (ab_re - 1.0) * li) / den
        hr_s[...] = h0r_ref[...]
        hi_s[...] = h0i_ref[...]

    u = u_ref[...]
    ub = u.astype(BF16)
    n_slab = bre_ref.shape[0]
    w_state = n_state // n_slab
    for m in range(n_slab):
        um = ub[:, m * LANES:(m + 1) * LANES]
        st = slice(m * w_state, (m + 1) * w_state)
        bu_re = _dot(um, bre_ref[m])
        bu_im = _dot(um, bim_ref[m])
        q_re = coef_ref[2:3, st]
        q_im = coef_ref[3:4, st]
        sre_ref[:, st] = q_re * bu_re - q_im * bu_im
        sim_ref[:, st] = q_re * bu_im + q_im * bu_re

    n_step = tm // bg
    for lc in range(n_state // S5_LANE_CHUNK):
        lanes = slice(lc * S5_LANE_CHUNK, (lc + 1) * S5_LANE_CHUNK)
        ab_re = jnp.broadcast_to(coef_ref[0:1, lanes], (SUBLANES, S5_LANE_CHUNK))
        ab_im = jnp.broadcast_to(coef_ref[1:2, lanes], (SUBLANES, S5_LANE_CHUNK))

        def strip(rb, _):
            r0 = pl.multiple_of(rb * SUBLANES, SUBLANES)

            def step(t, carry):
                s_re, s_im = carry
                rows = pl.ds(pl.multiple_of(t * bg + r0, SUBLANES), SUBLANES)
                n_re = ab_re * s_re - ab_im * s_im + sre_ref[rows, lanes]
                n_im = ab_re * s_im + ab_im * s_re + sim_ref[rows, lanes]
                sre_ref[rows, lanes] = n_re
                sim_ref[rows, lanes] = n_im
                return n_re, n_im

            rs = pl.ds(r0, SUBLANES)
            s_re, s_im = lax.fori_loop(0, n_step, step, (hr_s[rs, lanes], hi_s[rs, lanes]))
            hr_s[rs, lanes] = s_re
            hi_s[rs, lanes] = s_im
            return 0

        lax.fori_loop(0, bg // SUBLANES, strip, 0)

    hr_o[...] = hr_s[...]
    hi_o[...] = hi_s[...]
    y = jnp.concatenate(
        [_dot(sre_ref[:, m * w_state:(m + 1) * w_state].astype(BF16), cre_ref[m])
         - _dot(sim_ref[:, m * w_state:(m + 1) * w_state].astype(BF16), cim_ref[m]) for m in range(n_slab)], axis=1)
    y = y + d_ref[...] * u
    z = _dot(y.astype(BF16), wglu_ref[...])
    o_ref[...] = z[:, :d_model] * jax.nn.sigmoid(z[:, d_model:])


def _s5(u, h0r, h0i, prm, bg, tm):
    rows, c_c = u.shape
    n_state = h0r.shape[1]
    d_model = prm["w_glu"].shape[1] // 2
    consts = [prm["lam_re"], prm["lam_im"], prm["log_dt"], prm["b_re"], prm["b_im"], prm["c_re"], prm["c_im"],
              prm["d"], prm["w_glu"]]
    return pl.pallas_call(
        functools.partial(_s5_kernel, bg=bg),
        grid=(rows // tm,),
        in_specs=[pl.BlockSpec((tm, c_c), lambda i: (i, 0)), _const_spec(h0r.shape), _const_spec(h0i.shape)]
                 + [_const_spec(c.shape) for c in consts],
        out_specs=[pl.BlockSpec((tm, d_model), lambda i: (i, 0)), _const_spec(h0r.shape), _const_spec(h0i.shape)],
        out_shape=[jax.ShapeDtypeStruct((rows, d_model), F32), jax.ShapeDtypeStruct(h0r.shape, F32),
                   jax.ShapeDtypeStruct(h0i.shape, F32)],
        scratch_shapes=[pltpu.VMEM((tm, n_state), F32), pltpu.VMEM((tm, n_state), F32),
                        pltpu.VMEM((SUBLANES, n_state), F32), pltpu.VMEM(h0r.shape, F32),
                        pltpu.VMEM(h0i.shape, F32)],
        compiler_params=_params("arbitrary"),
        name="s5",
    )(u, h0r, h0i, *consts)


def _merge_kernel(y_ref, bonus_ref, g_ref, yb_ref, yc_ref, pg_ref, x_ref, lng_ref, lnb_ref, ones_ref, woa_ref,
                  wob_ref, bgate_ref, wout_ref, o_ref):
    d_model = x_ref.shape[1]
    ones_bd = ones_ref[...]
    y = y_ref[...]
    cen = y - _head_sum(y, ones_bd) * (1.0 / HEAD)
    var = _head_sum(cen * cen, ones_bd) * (1.0 / HEAD)
    yn = cen * lax.rsqrt(var + GN_EPS) * lng_ref[...] + lnb_ref[...]
    y_a = _dot(((yn + bonus_ref[...]) * g_ref[...]).astype(BF16), woa_ref[...])
    y_b = _dot(yb_ref[...], wob_ref[...])
    gates = jax.nn.sigmoid(pg_ref[...] + bgate_ref[...])
    h = (gates[:, :d_model] * y_a + gates[:, d_model:2 * d_model] * y_b + gates[:, 2 * d_model:] * yc_ref[...])
    o_ref[...] = x_ref[...] + _dot(h.astype(BF16), wout_ref[...])


def _merge(y, bonus, g, yb, yc, pg, x, prm, tm):
    rows, d_model = x.shape
    acts = [y, bonus, g, yb, yc, pg, x]
    consts = [prm["ln_g"], prm["ln_b"], prm["ones_bd"], prm["rwkv_w_out"], prm["conf_w_out"], prm["b_gate"],
              prm["w_out"]]
    return pl.pallas_call(
        _merge_kernel,
        grid=(rows // tm,),
        in_specs=[pl.BlockSpec((tm, a.shape[1]), lambda i: (i, 0)) for a in acts]
                 + [_const_spec(c.shape) for c in consts],
        out_specs=pl.BlockSpec((tm, d_model), lambda i: (i, 0)),
        out_shape=jax.ShapeDtypeStruct((rows, d_model), F32),
        compiler_params=_params("arbitrary"),
        name="merge",
    )(*acts, *consts)


def _ffn_kernel(x_ref, st1_ref, st2_ref, g_ref, wu1_ref, wu2_ref, dw1_ref, dw2_ref, db1_ref, db2_ref, wd_ref,
                gf_ref, o_ref, carry_ref, xn_ref, acc_ref, full1_ref, full2_ref, *, bg, n_tap, final_norm):
    i = pl.program_id(0)
    f = pl.program_id(1)
    tm = x_ref.shape[0]
    halo = (n_tap - 1) * bg

    @pl.when(f == 0)
    def _():
        xn_ref[...] = _rms_scale(x_ref[...], g_ref[...]).astype(BF16)
        acc_ref[...] = jnp.zeros_like(acc_ref)

    @pl.when(i == 0)
    def _():
        carry_ref[f, 0] = st1_ref[...]
        carry_ref[f, 1] = st2_ref[...]

    xn = xn_ref[...]
    halves = []
    for h, (wu_ref, dw_ref, db_ref, full_ref) in enumerate(
            ((wu1_ref, dw1_ref, db1_ref, full1_ref), (wu2_ref, dw2_ref, db2_ref, full2_ref))):
        full_ref[0:halo, :] = carry_ref[f, h]
        full_ref[halo:halo + tm, :] = _dot(xn, wu_ref[...])
        c = jnp.zeros((tm, full_ref.shape[1]), F32) + db_ref[...]
        for tap in range(n_tap):
            c = c + dw_ref[pl.ds(tap, 1), :] * full_ref[tap * bg:tap * bg + tm, :]
        carry_ref[f, h] = full_ref[tm:tm + halo, :]
        halves.append(c)
    act = (halves[0] * jax.nn.sigmoid(halves[0]) * halves[1]).astype(BF16)
    acc_ref[...] += _dot(act, wd_ref[...])

    @pl.when(f == pl.num_programs(1) - 1)
    def _():
        out = x_ref[...] + acc_ref[...]
        o_ref[...] = _rms_scale(out, gf_ref[...]) if final_norm else out


def _conv_ffn(x, st0, prm, final_g, bg, tm, final_norm):
    rows, d_model = x.shape
    d_ff = prm["w_down"].shape[0]
    n_tap = prm["dw_w"].shape[0]
    halo = (n_tap - 1) * bg
    n_f = 2
    fc = d_ff // n_f
    assert fc * n_f == d_ff and fc % LANES == 0
    lo = lambda i, f: (0, f)
    hi = lambda i, f: (0, f + n_f)
    out, carry = pl.pallas_call(
        functools.partial(_ffn_kernel, bg=bg, n_tap=n_tap, final_norm=final_norm),
        grid=(rows // tm, n_f),
        in_specs=[pl.BlockSpec((tm, d_model), lambda i, f: (i, 0)),
                  pl.BlockSpec((halo, fc), lo), pl.BlockSpec((halo, fc), hi),
                  _const_spec(prm["norm_g"].shape),
                  pl.BlockSpec((d_model, fc), lo), pl.BlockSpec((d_model, fc), hi),
                  pl.BlockSpec((n_tap, fc), lo), pl.BlockSpec((n_tap, fc), hi),
                  pl.BlockSpec((1, fc), lo), pl.BlockSpec((1, fc), hi),
                  pl.BlockSpec((fc, d_model), lambda i, f: (f, 0)),
                  _const_spec(final_g.shape)],
        out_specs=[pl.BlockSpec((tm, d_model), lambda i, f: (i, 0)), _const_spec((n_f, 2, halo, fc))],
        out_shape=[jax.ShapeDtypeStruct((rows, d_model), F32), jax.ShapeDtypeStruct((n_f, 2, halo, fc), F32)],
        scratch_shapes=[pltpu.VMEM((tm, d_model), BF16), pltpu.VMEM((tm, d_model), F32),
                        pltpu.VMEM((halo + tm, fc), F32), pltpu.VMEM((halo + tm, fc), F32)],
        compiler_params=_params("arbitrary", "arbitrary"),
        name="conv_ffn",
    )(x, st0, st0, prm["norm_g"], prm["w_up"], prm["w_up"], prm["dw_w"], prm["dw_w"], prm["dw_b"], prm["dw_b"],
      prm["w_down"], final_g)
    return out, carry.transpose(2, 1, 0, 3).reshape(halo, 2 * d_ff)


def _tile(rows, bg, pref):
    tm = max(bg, min(pref, rows))
    while rows % tm or tm % bg:
        tm -= bg
    return tm


def _time_major(st):
    return st.transpose(1, 0, 2).reshape(-1, st.shape[-1])


def _seq_major(st, bg):
    return st.reshape(-1, bg, st.shape[-1]).transpose(1, 0, 2)


def _run_trunk(x_bm, states, layers, final_g):
    bg, n_t, d_model = x_bm.shape
    rows = bg * n_t
    x = x_bm.transpose(1, 0, 2).reshape(rows, d_model)
    shift, wkv, conv, ssm_re, ssm_im, ffn_conv = states
    outs = ([], [], [], [], [], [])
    for l, prm in enumerate(layers):
        p_r, p_c, p_s, p_g = _in_proj(x, prm["norm_mix_g"], prm["w_in"], prm["in_widths"], _tile(rows, bg, 256))
        r, w, k, v, a, b, g, bonus = _rwkv_prep(p_r, shift[l], prm["rwkv"], bg, _tile(rows, bg, 512))
        y, wkv_n = _wkv_group(r, w, k, v, a, b, wkv[l], bg, n_t)
        yb, conv_n = _conformer(p_c, _time_major(conv[l]), prm["conf"], bg, _tile(rows, bg, 1024 if bg > 8 else 512))
        n_state = ssm_re.shape[2] * ssm_re.shape[3]
        yc, re_n, im_n = _s5(p_s, ssm_re[l].reshape(bg, n_state), ssm_im[l].reshape(bg, n_state), prm["s5"], bg,
                             _tile(rows, bg, 256 if bg > 8 else 512))
        x = _merge(y, bonus, g, yb, yc, p_g, x, prm["merge"], _tile(rows, bg, 512))
        x, ffn_n = _conv_ffn(x, _time_major(ffn_conv[l]), prm["ffn"], final_g, bg, _tile(rows, bg, 512),
                             final_norm=(l == len(layers) - 1))
        new = (p_r[rows - bg:], wkv_n, _seq_major(conv_n, bg), re_n.reshape(ssm_re.shape[1:]),
               im_n.reshape(ssm_im.shape[1:]), _seq_major(ffn_n, bg))
        for lst, s in zip(outs, new):
            lst.append(s)
    y = x.reshape(n_t, bg, d_model).transpose(1, 0, 2)
    return y, [jnp.stack(lst) for lst in outs]


def _block_diag(blocks):
    n_g, a, b = blocks.shape
    eye = jnp.eye(n_g, dtype=blocks.dtype)
    return (eye[:, None, :, None] * blocks[:, :, None, :]).reshape(n_g * a, n_g * b)


def _prep_layer(l, w):
    row = lambda v: v.reshape(1, -1).astype(F32)
    c_a = w["rwkv_w0"].shape[1]
    n_rwkv = w["rwkv_mu"].shape[1]
    c_b = w["conf_dw_b"].shape[1]
    c_c = w["s5_d"].shape[1]
    d_model = w["w_out"].shape[1]
    lora_w = w["rwkv_w_up"].shape[1]
    lora_a = w["rwkv_a_up"].shape[1]
    assert lora_w + lora_a == LORA_PAIR and n_rwkv == 3 * c_a + LORA_PAIR + w["rwkv_g_up"].shape[1]
    n_grp, p_c = w["s5_lambda_re"].shape[1:]
    grp_per_slab = LANES // (c_c // n_grp)

    def slabs(blocks):
        t = blocks.transpose(0, 2, 1)
        t = t.reshape(n_grp // grp_per_slab, grp_per_slab, *t.shape[1:])
        return jax.vmap(_block_diag)(t).astype(BF16)

    ones_bd = _block_diag(jnp.ones((c_a // HEAD, HEAD, HEAD), F32)).astype(BF16)
    rwkv = dict(
        mu=row(w["rwkv_mu"][l]), w0=row(w["rwkv_w0"][l]), a0=row(w["rwkv_a0"][l]), k_k=row(w["rwkv_k_k"][l]),
        k_a=row(w["rwkv_k_a"][l]), r_k=row(w["rwkv_r_k"][l]),
        w_up=jnp.concatenate([w["rwkv_w_up"][l], jnp.zeros((lora_a, c_a), F32)], axis=0).astype(BF16),
        a_up=jnp.concatenate([jnp.zeros((lora_w, c_a), F32), w["rwkv_a_up"][l]], axis=0).astype(BF16),
        g_up=w["rwkv_g_up"][l].astype(BF16), ones_bd=ones_bd)
    conf = dict(dw_w=w["conf_dw_w"][l], dw_b=row(w["conf_dw_b"][l]), ln_g=row(w["conf_ln_g"][l]),
                ln_b=row(w["conf_ln_b"][l]))
    s5 = dict(
        lam_re=row(w["s5_lambda_re"][l]), lam_im=row(w["s5_lambda_im"][l]),
        log_dt=row(jnp.repeat(w["s5_log_dt"][l], p_c)),
        b_re=slabs(w["s5_b_re"][l]), b_im=slabs(w["s5_b_im"][l]),
        c_re=slabs(w["s5_c_re"][l]), c_im=slabs(w["s5_c_im"][l]),
        d=row(w["s5_d"][l]), w_glu=w["s5_w_glu"][l].astype(BF16))
    merge = dict(ln_g=row(w["rwkv_ln_g"][l]), ln_b=row(w["rwkv_ln_b"][l]), ones_bd=ones_bd,
                 rwkv_w_out=w["rwkv_w_out"][l].astype(BF16), conf_w_out=w["conf_w_out"][l].astype(BF16),
                 b_gate=row(w["b_gate"][l]), w_out=w["w_out"][l].astype(BF16))
    ffn = dict(norm_g=row(w["norm_ffn_g"][l]), w_up=w["ffn_w_up"][l].astype(BF16), dw_w=w["ffn_dw_w"][l],
               dw_b=row(w["ffn_dw_b"][l]), w_down=w["ffn_w_down"][l].astype(BF16))
    in_widths = (n_rwkv, 2 * c_b, c_c, w["b_gate"].shape[1])
    assert sum(in_widths) == w["w_in"].shape[2] and d_model * 3 == in_widths[3] and n_grp * p_c > 0
    return dict(norm_mix_g=row(w["norm_mix_g"][l]), w_in=w["w_in"][l].astype(BF16), in_widths=in_widths,
                rwkv=rwkv, conf=conf, s5=s5, merge=merge, ffn=ffn)


def kernel(x_prompt, x_sample, state_shift, state_wkv, state_conv, state_ssm_re, state_ssm_im, state_ffn_conv, norm_mix_g, w_in, b_gate, rwkv_mu, rwkv_w0, rwkv_w_up, rwkv_a0, rwkv_a_up, rwkv_g_up, rwkv_k_k, rwkv_k_a, rwkv_r_k, rwkv_ln_g, rwkv_ln_b, rwkv_w_out, conf_dw_w, conf_dw_b, conf_ln_g, conf_ln_b, conf_w_out, s5_lambda_re, s5_lambda_im, s5_log_dt, s5_b_re, s5_b_im, s5_c_re, s5_c_im, s5_d, s5_w_glu, w_out, norm_ffn_g, ffn_w_up, ffn_dw_w, ffn_dw_b, ffn_w_down, norm_final_g):
    w = dict(norm_mix_g=norm_mix_g, w_in=w_in, b_gate=b_gate, rwkv_mu=rwkv_mu, rwkv_w0=rwkv_w0,
             rwkv_w_up=rwkv_w_up, rwkv_a0=rwkv_a0, rwkv_a_up=rwkv_a_up, rwkv_g_up=rwkv_g_up, rwkv_k_k=rwkv_k_k,
             rwkv_k_a=rwkv_k_a, rwkv_r_k=rwkv_r_k, rwkv_ln_g=rwkv_ln_g, rwkv_ln_b=rwkv_ln_b,
             rwkv_w_out=rwkv_w_out, conf_dw_w=conf_dw_w, conf_dw_b=conf_dw_b, conf_ln_g=conf_ln_g,
             conf_ln_b=conf_ln_b, conf_w_out=conf_w_out, s5_lambda_re=s5_lambda_re, s5_lambda_im=s5_lambda_im,
             s5_log_dt=s5_log_dt, s5_b_re=s5_b_re, s5_b_im=s5_b_im, s5_c_re=s5_c_re, s5_c_im=s5_c_im, s5_d=s5_d,
             s5_w_glu=s5_w_glu, w_out=w_out, norm_ffn_g=norm_ffn_g, ffn_w_up=ffn_w_up, ffn_dw_w=ffn_dw_w,
             ffn_dw_b=ffn_dw_b, ffn_w_down=ffn_w_down)
    depth = w_in.shape[0]
    layers = [_prep_layer(l, w) for l in range(depth)]
    final_g = norm_final_g.reshape(1, -1)
    bp = x_prompt.shape[0]
    zeros_like_state = lambda s: jnp.zeros((depth, bp) + s.shape[2:], s.dtype)
    prompt_states = tuple(zeros_like_state(s) for s in
                          (state_shift, state_wkv, state_conv, state_ssm_re, state_ssm_im, state_ffn_conv))
    y_prompt, p_states = _run_trunk(x_prompt, prompt_states, layers, final_g)
    y_sample, s_states = _run_trunk(
        x_sample, (state_shift, state_wkv, state_conv, state_ssm_re, state_ssm_im, state_ffn_conv), layers, final_g)
    return (y_prompt, y_sample, *p_states, *s_states)
```

```python
import functools
import math

import jax
import jax.numpy as jnp
from jax import lax
from jax.experimental import pallas as pl
from jax.experimental.pallas import tpu as pltpu

F32 = jnp.float32
BF16 = jnp.bfloat16

VMEM_LIMIT_BYTES = 56 * 1024 * 1024
SUBLANES = 8
LANES = 128
MXU_DIM = 256

HEAD = 64
LORA_PAIR = 128
RMS_EPS = 1e-6
LN_EPS = 1e-5
GN_EPS = 64e-5
DECAY_SCALE = math.exp(-0.5)


def _params(*sem):
    return pltpu.CompilerParams(dimension_semantics=sem, vmem_limit_bytes=VMEM_LIMIT_BYTES)


def _const_spec(shape):
    zeros = (0,) * len(shape)
    return pl.BlockSpec(shape, lambda *_: zeros)


def _dot(a, b):
    return jnp.dot(a, b, preferred_element_type=F32)


def _head_sum(x, ones_bd):
    hi = x.astype(BF16)
    lo = (x - hi.astype(F32)).astype(BF16)
    wd = ones_bd.shape[0]
    return jnp.concatenate(
        [_dot(hi[:, s:s + wd], ones_bd) + _dot(lo[:, s:s + wd], ones_bd) for s in range(0, x.shape[1], wd)], axis=1)


def _rms_scale(x, g):
    ms = jnp.mean(x * x, axis=-1, keepdims=True)
    return x * lax.rsqrt(ms + RMS_EPS) * g


def _in_proj_kernel(x_ref, g_ref, w_ref, s0_ref, mu_ref, w0_ref, a0_ref, kk_ref, ka_ref, rk_ref, wup_ref, aup_ref,
                    gup_ref, ones_ref, pc_o, ps_o, pg_o, r_o, w_o, k_o, v_o, a_o, b_o, g_o, bonus_o, shift_o,
                    carry_ref, *, bg, c_a, bounds):
    @pl.when(pl.program_id(0) == 0)
    def _():
        carry_ref[...] = s0_ref[...]

    xn = _rms_scale(x_ref[...], g_ref[...]).astype(BF16)
    for o_ref, (lo, hi) in zip((pc_o, ps_o, pg_o), bounds[1:]):
        o_ref[...] = _dot(xn, w_ref[:, lo:hi])
    p = _dot(xn, w_ref[:, bounds[0][0]:bounds[0][1]])
    tm = p.shape[0]
    prev = carry_ref[...] if tm == bg else jnp.concatenate([carry_ref[...], p[:tm - bg]], axis=0)
    carry_ref[...] = p[tm - bg:]
    shift_o[...] = p[tm - bg:]
    xm = p + (prev - p) * mu_ref[...]
    r = xm[:, 0:c_a]
    k = xm[:, c_a:2 * c_a]
    v = xm[:, 2 * c_a:3 * c_a]
    xwa = xm[:, 3 * c_a:3 * c_a + LORA_PAIR]
    xg = xm[:, 3 * c_a + LORA_PAIR:]
    ones_bd = ones_ref[...]
    z = w0_ref[...] + _dot(jnp.tanh(xwa).astype(BF16), wup_ref[...])
    decay = jnp.exp(-DECAY_SCALE * jax.nn.sigmoid(z))
    a = jax.nn.sigmoid(a0_ref[...] + _dot(xwa.astype(BF16), aup_ref[...]))
    g = _dot(jax.nn.sigmoid(xg).astype(BF16), gup_ref[...])
    kk = k * kk_ref[...]
    kk = kk * lax.rsqrt(jnp.maximum(_head_sum(kk * kk, ones_bd), 1e-24))
    k_h = k * (1.0 + (a - 1.0) * ka_ref[...])
    r_o[...] = r
    w_o[...] = decay
    k_o[...] = k_h
    v_o[...] = v
    a_o[...] = -kk
    b_o[...] = kk * a
    g_o[...] = g
    bonus_o[...] = _head_sum(r * k_h * rk_ref[...], ones_bd) * v


def _in_proj(x, g, w_bf16, widths, shift0, prm, bg, tm):
    rows, d = x.shape
    c_a = prm["w0"].shape[1]
    bounds, lo = [], 0
    for wd in widths:
        bounds.append((lo, lo + wd))
        lo += wd
    consts = [prm["mu"], prm["w0"], prm["a0"], prm["k_k"], prm["k_a"], prm["r_k"], prm["w_up"], prm["a_up"],
              prm["g_up"], prm["ones_bd"]]
    out_w = list(widths[1:]) + [c_a] * 8
    outs = pl.pallas_call(
        functools.partial(_in_proj_kernel, bg=bg, c_a=c_a, bounds=tuple(bounds)),
        grid=(rows // tm,),
        in_specs=[pl.BlockSpec((tm, d), lambda i: (i, 0)), _const_spec(g.shape),
                  pl.BlockSpec(w_bf16.shape, lambda i: (0, 0), pipeline_mode=pl.Buffered(1)),
                  _const_spec(shift0.shape)] + [_const_spec(c.shape) for c in consts],
        out_specs=[pl.BlockSpec((tm, wd), lambda i: (i, 0)) for wd in out_w] + [_const_spec(shift0.shape)],
        out_shape=[jax.ShapeDtypeStruct((rows, wd), F32) for wd in out_w]
                  + [jax.ShapeDtypeStruct(shift0.shape, F32)],
        scratch_shapes=[pltpu.VMEM(shift0.shape, F32)],
        compiler_params=_params("arbitrary"),
        name="in_proj",
    )(x, g, w_bf16, shift0, *consts)
    return outs[0], outs[1], outs[2], outs[3:11], outs[11]


def _wkv_kernel(r_ref, w_ref, k_ref, v_ref, a_ref, b_ref, s0_ref, y_ref, st_ref, *scratch, n_i, tc, nb, n_h, n_chunk):
    slots = (scratch[0:6], scratch[6:12])
    zy = scratch[12]
    dup = HEAD // n_i
    n_pair = tc // 2
    g = pl.program_id(1)
    low = lax.broadcasted_iota(jnp.int32, (nb, LANES), 1) < HEAD
    low_i = lax.broadcasted_iota(jnp.int32, (n_i, LANES), 1) < HEAD
    swap = lambda x: pltpu.roll(x, HEAD, 1)

    def to_chain_minor(p, slot):
        t0 = 2 * p
        for src_ref, dst_ref in zip((r_ref, w_ref, k_ref, v_ref, a_ref, b_ref), slots[slot]):
            pieces = []
            for q in range(n_h // 2):
                x0 = src_ref[t0, :, q * LANES:(q + 1) * LANES]
                x1 = src_ref[t0 + 1, :, q * LANES:(q + 1) * LANES]
                pieces.append(jnp.where(low, x0, swap(x1)))
                pieces.append(jnp.where(low, swap(x0), x1))
            tt = jnp.concatenate(pieces * dup, axis=0).T
            dst_ref[t0] = tt[:HEAD]
            dst_ref[t0 + 1] = tt[HEAD:]

    def row(ref, t, j):
        return ref[t, pl.ds(j, 1), :]

    def step(t, sa, slot):
        zr, zw, zk, zv, za, zb = slots[slot]
        t_next = jnp.minimum(t + 1, tc - 1)
        v = zv[t] if dup == 1 else jnp.where(low_i, zv[t, :n_i, :], zv[t, n_i:, :])
        y = jnp.zeros((n_i, LANES), F32)
        sa_next = jnp.zeros((n_i, LANES), F32)
        for j in range(HEAD):
            s = st_ref[j] * row(zw, t, j) + sa * row(zb, t, j) + v * row(zk, t, j)
            st_ref[j] = s
            y = y + s * row(zr, t, j)
            sa_next = sa_next + s * row(za, t_next, j)
        zy[t] = y
        return sa_next

    def y_pair(p):
        t0 = 2 * p
        ys = []
        for t in (t0, t0 + 1):
            y = zy[t]
            ys += [y] if dup == 1 else [y, swap(y)]
        wt = jnp.concatenate(ys, axis=0).T
        for q in range(n_h // 2):
            even = wt[2 * q * nb:(2 * q + 1) * nb]
            odd = wt[(2 * q + 1) * nb:(2 * q + 2) * nb]
            y_ref[t0, :, q * LANES:(q + 1) * LANES] = jnp.where(low, even, swap(odd))
            y_ref[t0 + 1, :, q * LANES:(q + 1) * LANES] = jnp.where(low, swap(even), odd)

    def recurrence(slot, with_relayout):
        za = slots[slot][4]
        sa0 = jnp.zeros((n_i, LANES), F32)
        for j in range(HEAD):
            sa0 = sa0 + st_ref[j] * row(za, 0, j)

        def body(p, sa):
            sa = step(2 * p, sa, slot)
            if with_relayout:
                to_chain_minor(p, 1 - slot)
            y_pair(jnp.maximum(p - 1, 0))
            return step(2 * p + 1, sa, slot)

        lax.fori_loop(0, n_pair, body, sa0)
        y_pair(n_pair - 1)

    @pl.when(g == 0)
    def _():
        st_ref[...] = s0_ref[...]
        zy[...] = jnp.zeros_like(zy)

        def first(p, _):
            to_chain_minor(p, 0)
            return 0
        lax.fori_loop(0, n_pair, first, 0, unroll=2)

    for parity in range(2 if n_chunk > 1 else 0):
        @pl.when(jnp.logical_and(jnp.logical_and(g > 0, g < n_chunk), g % 2 == parity))
        def _():
            recurrence(1 - parity, True)

    @pl.when(g == n_chunk)
    def _():
        recurrence((n_chunk - 1) % 2, False)


def _wkv_group(r, w, k, v, a, b, wkv0, bg, n_t):
    c_a = r.shape[1]
    n_h = c_a // HEAD
    dup = max(1, LANES // (bg * n_h))
    nb = LANES // (n_h * dup)
    n_i = HEAD // dup
    n_blk = bg // nb
    tc = min(n_t, 64)
    assert n_h % 2 == 0 and nb * n_blk == bg and nb % SUBLANES == 0 and tc % 2 == 0 and n_t % tc == 0
    s0 = wkv0.reshape(n_blk, nb, n_h, dup, n_i, HEAD).transpose(5, 4, 0, 3, 2, 1).reshape(HEAD, n_i, n_blk * LANES)
    n_chunk = n_t // tc
    tok_in = pl.BlockSpec((tc, nb, c_a), lambda c, g: (jnp.minimum(g, n_chunk - 1), c, 0))
    tok_out = pl.BlockSpec((tc, nb, c_a), lambda c, g: (jnp.maximum(g - 1, 0), c, 0))
    state = pl.BlockSpec((HEAD, n_i, LANES), lambda c, g: (0, 0, c))
    y, st = pl.pallas_call(
        functools.partial(_wkv_kernel, n_i=n_i, tc=tc, nb=nb, n_h=n_h, n_chunk=n_chunk),
        grid=(n_blk, n_chunk + 1),
        in_specs=[tok_in] * 6 + [state],
        out_specs=[tok_out, state],
        out_shape=[jax.ShapeDtypeStruct((n_t, bg, c_a), F32), jax.ShapeDtypeStruct(s0.shape, F32)],
        scratch_shapes=[pltpu.VMEM((tc, HEAD, LANES), F32)] * 12 + [pltpu.VMEM((tc, n_i, LANES), F32)],
        compiler_params=_params("arbitrary", "arbitrary"),
        name="wkv",
    )(*(x.reshape(n_t, bg, c_a) for x in (r, w, k, v, a, b)), s0)
    return y.reshape(n_t * bg, c_a), st


def _wkv_states_from_chains(st, bg, n_h):
    depth, _, n_i, n_c = st.shape
    dup = HEAD // n_i
    nb = LANES // (n_h * dup)
    st = st.reshape(depth, HEAD, n_i, n_c // LANES, dup, n_h, nb).transpose(0, 3, 6, 5, 4, 2, 1)
    return st.reshape(depth, bg, n_h, HEAD, HEAD)


def _conf_kernel(p_ref, st0_ref, dww_ref, dwb_ref, lng_ref, lnb_ref, o_ref, st_o_ref, full_ref, *, bg, n_tap,
                 carry):
    tm, c2 = p_ref.shape
    c_b = c2 // 2
    halo = (n_tap - 1) * bg

    @pl.when(pl.program_id(0) == 0)
    def _():
        full_ref[0:halo, :] = st0_ref[...]

    p = p_ref[...]
    full_ref[halo:halo + tm, :] = p[:, :c_b] * jax.nn.sigmoid(p[:, c_b:])
    acc = jnp.zeros((tm, c_b), F32) + dwb_ref[...]
    for tap in range(n_tap):
        acc = acc + dww_ref[pl.ds(tap, 1), :] * full_ref[tap * bg:tap * bg + tm, :]
    mean = jnp.mean(acc, axis=-1, keepdims=True)
    cen = acc - mean
    var = jnp.mean(cen * cen, axis=-1, keepdims=True)
    n = cen * lax.rsqrt(var + LN_EPS) * lng_ref[...] + lnb_ref[...]
    o_ref[...] = (n * jax.nn.sigmoid(n)).astype(o_ref.dtype)
    tail = full_ref[tm:tm + halo, :]
    st_o_ref[...] = tail
    if carry:
        full_ref[0:halo, :] = tail


def _conformer(p_c, st0, prm, bg, tm):
    rows, c2 = p_c.shape
    c_b = c2 // 2
    n_tap = prm["dw_w"].shape[0]
    halo = (n_tap - 1) * bg
    consts = [prm["dw_w"], prm["dw_b"], prm["ln_g"], prm["ln_b"]]
    return pl.pallas_call(
        functools.partial(_conf_kernel, bg=bg, n_tap=n_tap, carry=rows > tm),
        grid=(rows // tm,),
        in_specs=[pl.BlockSpec((tm, c2), lambda i: (i, 0)), _const_spec(st0.shape)]
                 + [_const_spec(c.shape) for c in consts],
        out_specs=[pl.BlockSpec((tm, c_b), lambda i: (i, 0)), _const_spec((halo, c_b))],
        out_shape=[jax.ShapeDtypeStruct((rows, c_b), BF16), jax.ShapeDtypeStruct((halo, c_b), F32)],
        scratch_shapes=[pltpu.VMEM((halo + tm, c_b), F32)],
        compiler_params=_params("arbitrary"),
        name="conformer",
    )(p_c, st0, *consts)


S5_LANE_CHUNK = 512


def _s5_kernel(u_ref, h0r_ref, h0i_ref, lamr_ref, lami_ref, dt_ref, bre_ref, bim_ref, cre_ref, cim_ref, d_ref,
               wglu_ref, o_ref, hr_o, hi_o, sre_ref, sim_ref, coef_ref, hr_s, hi_s, *, bg):
    tm = u_ref.shape[0]
    n_state = sre_ref.shape[1]
    d_model = o_ref.shape[1]

    @pl.when(pl.program_id(0) == 0)
    def _():
        lr = lamr_ref[...]
        li = lami_ref[...]
        dt = jnp.exp(dt_ref[...])
        mag = jnp.exp(lr * dt)
        ab_re = mag * jnp.cos(li * dt)
        ab_im = mag * jnp.sin(li * dt)
        den = lr * lr + li * li
        coef_ref[0:1, :] = ab_re
        coef_ref[1:2, :] = ab_im
        coef_ref[2:3, :] = ((ab_re - 1.0) * lr + ab_im * li) / den
        coef_ref[3:4, :] = (ab_im * lr - (ab_re - 1.0) * li) / den
        hr_s[...] = h0r_ref[...]
        hi_s[...] = h0i_ref[...]

    u = u_ref[...]
    ub = u.astype(BF16)
    n_slab = bre_ref.shape[0]
    w_state = n_state // n_slab
    for m in range(n_slab):
        um = ub[:, m * LANES:(m + 1) * LANES]
        st = slice(m * w_state, (m + 1) * w_state)
        bu_re = _dot(um, bre_ref[m])
        bu_im = _dot(um, bim_ref[m])
        q_re = coef_ref[2:3, st]
        q_im = coef_ref[3:4, st]
        sre_ref[:, st] = q_re * bu_re - q_im * bu_im
        sim_ref[:, st] = q_re * bu_im + q_im * bu_re

    n_step = tm // bg
    for lc in range(n_state // S5_LANE_CHUNK):
        lanes = slice(lc * S5_LANE_CHUNK, (lc + 1) * S5_LANE_CHUNK)
        ab_re = jnp.broadcast_to(coef_ref[0:1, lanes], (SUBLANES, S5_LANE_CHUNK))
        ab_im = jnp.broadcast_to(coef_ref[1:2, lanes], (SUBLANES, S5_LANE_CHUNK))

        def strip(rb, _):
            r0 = pl.multiple_of(rb * SUBLANES, SUBLANES)

            def step(t, carry):
                s_re, s_im = carry
                rows = pl.ds(pl.multiple_of(t * bg + r0, SUBLANES), SUBLANES)
                n_re = ab_re * s_re - ab_im * s_im + sre_ref[rows, lanes]
                n_im = ab_re * s_im + ab_im * s_re + sim_ref[rows, lanes]
                sre_ref[rows, lanes] = n_re
                sim_ref[rows, lanes] = n_im
                return n_re, n_im

            rs = pl.ds(r0, SUBLANES)
            s_re, s_im = lax.fori_loop(0, n_step, step, (hr_s[rs, lanes], hi_s[rs, lanes]))
            hr_s[rs, lanes] = s_re
            hi_s[rs, lanes] = s_im
            return 0

        lax.fori_loop(0, bg // SUBLANES, strip, 0)

    hr_o[...] = hr_s[...]
    hi_o[...] = hi_s[...]
    y = jnp.concatenate(
        [_dot(sre_ref[:, m * w_state:(m + 1) * w_state].astype(BF16), cre_ref[m])
         - _dot(sim_ref[:, m * w_state:(m + 1) * w_state].astype(BF16), cim_ref[m]) for m in range(n_slab)], axis=1)
    y = y + d_ref[...] * u
    z = _dot(y.astype(BF16), wglu_ref[...])
    o_ref[...] = z[:, :d_model] * jax.nn.sigmoid(z[:, d_model:])


def _s5(u, h0r, h0i, prm, bg, tm):
    rows, c_c = u.shape
    n_state = h0r.shape[1]
    d_model = prm["w_glu"].shape[1] // 2
    consts = [prm["lam_re"], prm["lam_im"], prm["log_dt"], prm["b_re"], prm["b_im"], prm["c_re"], prm["c_im"],
              prm["d"], prm["w_glu"]]
    return pl.pallas_call(
        functools.partial(_s5_kernel, bg=bg),
        grid=(rows // tm,),
        in_specs=[pl.BlockSpec((tm, c_c), lambda i: (i, 0)), _const_spec(h0r.shape), _const_spec(h0i.shape)]
                 + [_const_spec(c.shape) for c in consts],
        out_specs=[pl.BlockSpec((tm, d_model), lambda i: (i, 0)), _const_spec(h0r.shape), _const_spec(h0i.shape)],
        out_shape=[jax.ShapeDtypeStruct((rows, d_model), F32), jax.ShapeDtypeStruct(h0r.shape, F32),
                   jax.ShapeDtypeStruct(h0i.shape, F32)],
        scratch_shapes=[pltpu.VMEM((tm, n_state), F32), pltpu.VMEM((tm, n_state), F32),
                        pltpu.VMEM((SUBLANES, n_state), F32), pltpu.VMEM(h0r.shape, F32),
                        pltpu.VMEM(h0i.shape, F32)],
        compiler_params=_params("arbitrary"),
        name="s5",
    )(u, h0r, h0i, *consts)


def _merge_kernel(y_ref, bonus_ref, g_ref, yb_ref, yc_ref, pg_ref, x_ref, lng_ref, lnb_ref, ones_ref, woa_ref,
                  wob_ref, bgate_ref, wout_ref, o_ref):
    d_model = x_ref.shape[1]
    ones_bd = ones_ref[...]
    y = y_ref[...]
    cen = y - _head_sum(y, ones_bd) * (1.0 / HEAD)
    var = _head_sum(cen * cen, ones_bd) * (1.0 / HEAD)
    yn = cen * lax.rsqrt(var + GN_EPS) * lng_ref[...] + lnb_ref[...]
    y_a = _dot(((yn + bonus_ref[...]) * g_ref[...]).astype(BF16), woa_ref[...])
    y_b = _dot(yb_ref[...], wob_ref[...])
    gates = jax.nn.sigmoid(pg_ref[...] + bgate_ref[...])
    h = (gates[:, :d_model] * y_a + gates[:, d_model:2 * d_model] * y_b + gates[:, 2 * d_model:] * yc_ref[...])
    o_ref[...] = x_ref[...] + _dot(h.astype(BF16), wout_ref[...])


def _merge(y, bonus, g, yb, yc, pg, x, prm, tm):
    rows, d_model = x.shape
    acts = [y, bonus, g, yb, yc, pg, x]
    consts = [prm["ln_g"], prm["ln_b"], prm["ones_bd"], prm["rwkv_w_out"], prm["conf_w_out"], prm["b_gate"],
              prm["w_out"]]
    return pl.pallas_call(
        _merge_kernel,
        grid=(rows // tm,),
        in_specs=[pl.BlockSpec((tm, a.shape[1]), lambda i: (i, 0)) for a in acts]
                 + [_const_spec(c.shape) for c in consts],
        out_specs=pl.BlockSpec((tm, d_model), lambda i: (i, 0)),
        out_shape=jax.ShapeDtypeStruct((rows, d_model), F32),
        compiler_params=_params("arbitrary"),
        name="merge",
    )(*acts, *consts)


def _ffn_kernel(x_ref, st_ref, g_ref, wu_ref, dw_ref, db_ref, wd_ref, gf_ref, o_ref, carry_ref, *full_refs, bg, n_tap,
                d_ff, final_norm):
    tm = x_ref.shape[0]
    halo = (n_tap - 1) * bg

    @pl.when(pl.program_id(0) == 0)
    def _():
        carry_ref[...] = st_ref[...]

    x = x_ref[...]
    xn = _rms_scale(x, g_ref[...]).astype(BF16)
    acts = []
    for ci, lo in enumerate(range(0, d_ff, MXU_DIM)):
        wd = min(MXU_DIM, d_ff - lo)
        halves = []
        for h in range(2):
            cols = slice(h * d_ff + lo, h * d_ff + lo + wd)
            full_ref = full_refs[(2 * ci + h) % len(full_refs)]
            full_ref[0:halo, 0:wd] = carry_ref[:, cols]
            full_ref[halo:halo + tm, 0:wd] = _dot(xn, wu_ref[:, cols])
            c = jnp.zeros((tm, wd), F32) + db_ref[:, cols]
            for tap in range(n_tap):
                c = c + dw_ref[pl.ds(tap, 1), cols] * full_ref[tap * bg:tap * bg + tm, 0:wd]
            carry_ref[:, cols] = full_ref[tm:tm + halo, 0:wd]
            halves.append(c)
        acts.append((halves[0] * jax.nn.sigmoid(halves[0]) * halves[1]).astype(BF16))
    out = x + _dot(jnp.concatenate(acts, axis=1), wd_ref[...])
    o_ref[...] = _rms_scale(out, gf_ref[...]) if final_norm else out


FFN_STAGING_BUFFERS = 4


def _conv_ffn(x, st0, prm, final_g, bg, tm, final_norm):
    rows, d_model = x.shape
    d_ff = prm["w_down"].shape[0]
    n_tap = prm["dw_w"].shape[0]
    halo = (n_tap - 1) * bg
    resident = lambda a: pl.BlockSpec(a.shape, lambda i: (0,) * a.ndim, pipeline_mode=pl.Buffered(1))
    return pl.pallas_call(
        functools.partial(_ffn_kernel, bg=bg, n_tap=n_tap, d_ff=d_ff, final_norm=final_norm),
        grid=(rows // tm,),
        in_specs=[pl.BlockSpec((tm, d_model), lambda i: (i, 0)), resident(st0), _const_spec(prm["norm_g"].shape),
                  resident(prm["w_up"]), _const_spec(prm["dw_w"].shape), _const_spec(prm["dw_b"].shape),
                  resident(prm["w_down"]), _const_spec(final_g.shape)],
        out_specs=[pl.BlockSpec((tm, d_model), lambda i: (i, 0)), _const_spec(st0.shape)],
        out_shape=[jax.ShapeDtypeStruct((rows, d_model), F32), jax.ShapeDtypeStruct(st0.shape, F32)],
        scratch_shapes=[pltpu.VMEM((halo + tm, MXU_DIM), F32)] * FFN_STAGING_BUFFERS,
        compiler_params=_params("arbitrary"),
        name="conv_ffn",
    )(x, st0, prm["norm_g"], prm["w_up"], prm["dw_w"], prm["dw_b"], prm["w_down"], final_g)


def _tile(rows, bg, pref):
    tm = max(bg, min(pref, rows))
    while rows % tm or tm % bg:
        tm -= bg
    return tm


def _time_major(st):
    return st.transpose(1, 0, 2).reshape(-1, st.shape[-1])


def _run_trunk(x_bm, states, layers, final_g):
    bg, n_t, d_model = x_bm.shape
    rows = bg * n_t
    x = x_bm.transpose(1, 0, 2).reshape(rows, d_model)
    shift, wkv, conv, ssm_re, ssm_im, ffn_conv = states
    outs = ([], [], [], [], [], [])
    for l, prm in enumerate(layers):
        p_c, p_s, p_g, (r, w, k, v, a, b, g, bonus), shift_n = _in_proj(
            x, prm["norm_mix_g"], prm["w_in"], prm["in_widths"], shift[l], prm["rwkv"], bg, _tile(rows, bg, 256))
        y, wkv_n = _wkv_group(r, w, k, v, a, b, wkv[l], bg, n_t)
        yb, conv_n = _conformer(p_c, _time_major(conv[l]), prm["conf"], bg, _tile(rows, bg, 1024 if bg > 8 else 512))
        n_state = ssm_re.shape[2] * ssm_re.shape[3]
        yc, re_n, im_n = _s5(p_s, ssm_re[l].reshape(bg, n_state), ssm_im[l].reshape(bg, n_state), prm["s5"], bg,
                             _tile(rows, bg, 256 if bg > 8 else 512))
        x = _merge(y, bonus, g, yb, yc, p_g, x, prm["merge"], _tile(rows, bg, 512))
        x, ffn_n = _conv_ffn(x, _time_major(ffn_conv[l]), prm["ffn"], final_g, bg, _tile(rows, bg, 512),
                             final_norm=(l == len(layers) - 1))
        for lst, s in zip(outs, (shift_n, wkv_n, conv_n, re_n, im_n, ffn_n)):
            lst.append(s)
    y = x.reshape(n_t, bg, d_model).transpose(1, 0, 2)
    shift_n, wkv_n, conv_n, re_n, im_n, ffn_n = (jnp.stack(lst) for lst in outs)
    depth = len(layers)
    new_states = [shift_n, _wkv_states_from_chains(wkv_n, bg, wkv.shape[2]),
                  conv_n.reshape(depth, -1, bg, conv_n.shape[-1]).transpose(0, 2, 1, 3),
                  re_n.reshape(ssm_re.shape), im_n.reshape(ssm_im.shape),
                  ffn_n.reshape(depth, -1, bg, ffn_n.shape[-1]).transpose(0, 2, 1, 3)]
    return y, new_states


def _block_diag(blocks):
    n_g, a, b = blocks.shape
    eye = jnp.eye(n_g, dtype=blocks.dtype)
    return (eye[:, None, :, None] * blocks[:, :, None, :]).reshape(n_g * a, n_g * b)


def _prep_layer(l, w):
    row = lambda v: v.reshape(1, -1).astype(F32)
    c_a = w["rwkv_w0"].shape[1]
    n_rwkv = w["rwkv_mu"].shape[1]
    c_b = w["conf_dw_b"].shape[1]
    c_c = w["s5_d"].shape[1]
    d_model = w["w_out"].shape[1]
    lora_w = w["rwkv_w_up"].shape[1]
    lora_a = w["rwkv_a_up"].shape[1]
    assert lora_w + lora_a == LORA_PAIR and n_rwkv == 3 * c_a + LORA_PAIR + w["rwkv_g_up"].shape[1]
    n_grp, p_c = w["s5_lambda_re"].shape[1:]
    grp_per_slab = LANES // (c_c // n_grp)

    def slabs(blocks):
        t = blocks.transpose(0, 2, 1)
        t = t.reshape(n_grp // grp_per_slab, grp_per_slab, *t.shape[1:])
        return jax.vmap(_block_diag)(t).astype(BF16)

    assert c_a % MXU_DIM == 0
    ones_bd = _block_diag(jnp.ones((MXU_DIM // HEAD, HEAD, HEAD), F32)).astype(BF16)
    rwkv = dict(
        mu=row(w["rwkv_mu"][l]), w0=row(w["rwkv_w0"][l]), a0=row(w["rwkv_a0"][l]), k_k=row(w["rwkv_k_k"][l]),
        k_a=row(w["rwkv_k_a"][l]), r_k=row(w["rwkv_r_k"][l]),
        w_up=jnp.concatenate([w["rwkv_w_up"][l], jnp.zeros((lora_a, c_a), F32)], axis=0).astype(BF16),
        a_up=jnp.concatenate([jnp.zeros((lora_w, c_a), F32), w["rwkv_a_up"][l]], axis=0).astype(BF16),
        g_up=w["rwkv_g_up"][l].astype(BF16), ones_bd=ones_bd)
    conf = dict(dw_w=w["conf_dw_w"][l], dw_b=row(w["conf_dw_b"][l]), ln_g=row(w["conf_ln_g"][l]),
                ln_b=row(w["conf_ln_b"][l]))
    s5 = dict(
        lam_re=row(w["s5_lambda_re"][l]), lam_im=row(w["s5_lambda_im"][l]),
        log_dt=row(jnp.repeat(w["s5_log_dt"][l], p_c)),
        b_re=slabs(w["s5_b_re"][l]), b_im=slabs(w["s5_b_im"][l]),
        c_re=slabs(w["s5_c_re"][l]), c_im=slabs(w["s5_c_im"][l]),
        d=row(w["s5_d"][l]), w_glu=w["s5_w_glu"][l].astype(BF16))
    merge = dict(ln_g=row(w["rwkv_ln_g"][l]), ln_b=row(w["rwkv_ln_b"][l]), ones_bd=ones_bd,
                 rwkv_w_out=w["rwkv_w_out"][l].astype(BF16), conf_w_out=w["conf_w_out"][l].astype(BF16),
                 b_gate=row(w["b_gate"][l]), w_out=w["w_out"][l].astype(BF16))
    ffn = dict(norm_g=row(w["norm_ffn_g"][l]), w_up=w["ffn_w_up"][l].astype(BF16), dw_w=w["ffn_dw_w"][l],
               dw_b=row(w["ffn_dw_b"][l]), w_down=w["ffn_w_down"][l].astype(BF16))
    in_widths = (n_rwkv, 2 * c_b, c_c, w["b_gate"].shape[1])
    assert sum(in_widths) == w["w_in"].shape[2] and d_model * 3 == in_widths[3] and n_grp * p_c > 0
    return dict(norm_mix_g=row(w["norm_mix_g"][l]), w_in=w["w_in"][l].astype(BF16), in_widths=in_widths,
                rwkv=rwkv, conf=conf, s5=s5, merge=merge, ffn=ffn)


def kernel(x_prompt, x_sample, state_shift, state_wkv, state_conv, state_ssm_re, state_ssm_im, state_ffn_conv, norm_mix_g, w_in, b_gate, rwkv_mu, rwkv_w0, rwkv_w_up, rwkv_a0, rwkv_a_up, rwkv_g_up, rwkv_k_k, rwkv_k_a, rwkv_r_k, rwkv_ln_g, rwkv_ln_b, rwkv_w_out, conf_dw_w, conf_dw_b, conf_ln_g, conf_ln_b, conf_w_out, s5_lambda_re, s5_lambda_im, s5_log_dt, s5_b_re, s5_b_im, s5_c_re, s5_c_im, s5_d, s5_w_glu, w_out, norm_ffn_g, ffn_w_up, ffn_dw_w, ffn_dw_b, ffn_w_down, norm_final_g):
    w = dict(norm_mix_g=norm_mix_g, w_in=w_in, b_gate=b_gate, rwkv_mu=rwkv_mu, rwkv_w0=rwkv_w0,
             rwkv_w_up=rwkv_w_up, rwkv_a0=rwkv_a0, rwkv_a_up=rwkv_a_up, rwkv_g_up=rwkv_g_up, rwkv_k_k=rwkv_k_k,
             rwkv_k_a=rwkv_k_a, rwkv_r_k=rwkv_r_k, rwkv_ln_g=rwkv_ln_g, rwkv_ln_b=rwkv_ln_b,
             rwkv_w_out=rwkv_w_out, conf_dw_w=conf_dw_w, conf_dw_b=conf_dw_b, conf_ln_g=conf_ln_g,
             conf_ln_b=conf_ln_b, conf_w_out=conf_w_out, s5_lambda_re=s5_lambda_re, s5_lambda_im=s5_lambda_im,
             s5_log_dt=s5_log_dt, s5_b_re=s5_b_re, s5_b_im=s5_b_im, s5_c_re=s5_c_re, s5_c_im=s5_c_im, s5_d=s5_d,
             s5_w_glu=s5_w_glu, w_out=w_out, norm_ffn_g=norm_ffn_g, ffn_w_up=ffn_w_up, ffn_dw_w=ffn_dw_w,
             ffn_dw_b=ffn_dw_b, ffn_w_down=ffn_w_down)
    depth = w_in.shape[0]
    layers = [_prep_layer(l, w) for l in range(depth)]
    final_g = norm_final_g.reshape(1, -1)
    bp = x_prompt.shape[0]
    zeros_like_state = lambda s: jnp.zeros((depth, bp) + s.shape[2:], s.dtype)
    prompt_states = tuple(zeros_like_state(s) for s in
                          (state_shift, state_wkv, state_conv, state_ssm_re, state_ssm_im, state_ffn_conv))
    y_prompt, p_states = _run_trunk(x_prompt, prompt_states, layers, final_g)
    y_sample, s_states = _run_trunk(
        x_sample, (state_shift, state_wkv, state_conv, state_ssm_re, state_ssm_im, state_ffn_conv), layers, final_g)
    return (y_prompt, y_sample, *p_states, *s_states)
```

```python
import functools
import math

import jax
import jax.numpy as jnp
from jax import lax
from jax.experimental import pallas as pl
from jax.experimental.pallas import tpu as pltpu

F32 = jnp.float32
BF16 = jnp.bfloat16

VMEM_LIMIT_BYTES = 56 * 1024 * 1024
SUBLANES = 8
LANES = 128
MXU_DIM = 256

HEAD = 64
LORA_PAIR = 128
RMS_EPS = 1e-6
LN_EPS = 1e-5
GN_EPS = 64e-5
DECAY_SCALE = math.exp(-0.5)


def _params(*sem):
    return pltpu.CompilerParams(dimension_semantics=sem, vmem_limit_bytes=VMEM_LIMIT_BYTES)


def _const_spec(shape):
    zeros = (0,) * len(shape)
    return pl.BlockSpec(shape, lambda *_: zeros)


def _dot(a, b):
    return jnp.dot(a, b, preferred_element_type=F32)


def _head_sum(x, ones_bd):
    hi = x.astype(BF16)
    lo = (x - hi.astype(F32)).astype(BF16)
    wd = ones_bd.shape[0]
    return jnp.concatenate(
        [_dot(hi[:, s:s + wd], ones_bd) + _dot(lo[:, s:s + wd], ones_bd) for s in range(0, x.shape[1], wd)], axis=1)


def _rms_scale(x, g):
    ms = jnp.mean(x * x, axis=-1, keepdims=True)
    return x * lax.rsqrt(ms + RMS_EPS) * g


def _chain_geometry(bg, n_h):
    dup = max(1, LANES // (bg * n_h))
    nb = LANES // (n_h * dup)
    assert n_h % 2 == 0 and nb % SUBLANES == 0 and bg % nb == 0
    return dup, nb, HEAD // dup, bg // nb


def _swap_halves(x):
    return pltpu.roll(x, HEAD, 1)


def _to_chain_minor(x, dst_ref, bg, n_h):
    dup, nb, _, n_blk = _chain_geometry(bg, n_h)
    low = lax.broadcasted_iota(jnp.int32, (nb, LANES), 1) < HEAD
    for t0 in range(0, x.shape[0] // bg, 2):
        for c in range(n_blk):
            r0 = t0 * bg + c * nb
            r1 = r0 + bg
            pieces = []
            for q in range(n_h // 2):
                x0 = x[r0:r0 + nb, q * LANES:(q + 1) * LANES]
                x1 = x[r1:r1 + nb, q * LANES:(q + 1) * LANES]
                pieces.append(jnp.where(low, x0, _swap_halves(x1)))
                pieces.append(jnp.where(low, _swap_halves(x0), x1))
            tt = jnp.concatenate(pieces * dup, axis=0).T
            dst_ref[t0, :, c * LANES:(c + 1) * LANES] = tt[:HEAD]
            dst_ref[t0 + 1, :, c * LANES:(c + 1) * LANES] = tt[HEAD:]


def _from_chain_minor(src_ref, bg, n_h):
    dup, nb, _, n_blk = _chain_geometry(bg, n_h)
    n_step = src_ref.shape[0]
    low = lax.broadcasted_iota(jnp.int32, (nb, LANES), 1) < HEAD
    tiles = {}
    for t0 in range(0, n_step, 2):
        for c in range(n_blk):
            ys = []
            for t in (t0, t0 + 1):
                y = src_ref[t, :, c * LANES:(c + 1) * LANES]
                ys += [y] if dup == 1 else [y, _swap_halves(y)]
            wt = jnp.concatenate(ys, axis=0).T
            first, second = [], []
            for q in range(n_h // 2):
                even = wt[2 * q * nb:(2 * q + 1) * nb]
                odd = wt[(2 * q + 1) * nb:(2 * q + 2) * nb]
                first.append(jnp.where(low, even, _swap_halves(odd)))
                second.append(jnp.where(low, _swap_halves(even), odd))
            tiles[t0, c] = jnp.concatenate(first, axis=1)
            tiles[t0 + 1, c] = jnp.concatenate(second, axis=1)
    return jnp.concatenate([tiles[t, c] for t in range(n_step) for c in range(n_blk)], axis=0)


def _in_proj_kernel(x_ref, g_ref, w_ref, s0_ref, mu_ref, w0_ref, a0_ref, kk_ref, ka_ref, rk_ref, wup_ref, aup_ref,
                    gup_ref, ones_ref, pc_o, ps_o, pg_o, g_o, bonus_o, r_o, w_o, k_o, v_o, a_o, b_o, shift_o,
                    carry_ref, *, bg, c_a, bounds):
    @pl.when(pl.program_id(0) == 0)
    def _():
        carry_ref[...] = s0_ref[...]

    xn = _rms_scale(x_ref[...], g_ref[...]).astype(BF16)
    for o_ref, (lo, hi) in zip((pc_o, ps_o, pg_o), bounds[1:]):
        o_ref[...] = _dot(xn, w_ref[:, lo:hi])
    p = _dot(xn, w_ref[:, bounds[0][0]:bounds[0][1]])
    tm = p.shape[0]
    prev = carry_ref[...] if tm == bg else jnp.concatenate([carry_ref[...], p[:tm - bg]], axis=0)
    carry_ref[...] = p[tm - bg:]
    shift_o[...] = p[tm - bg:]
    xm = p + (prev - p) * mu_ref[...]
    r = xm[:, 0:c_a]
    k = xm[:, c_a:2 * c_a]
    v = xm[:, 2 * c_a:3 * c_a]
    xwa = xm[:, 3 * c_a:3 * c_a + LORA_PAIR]
    xg = xm[:, 3 * c_a + LORA_PAIR:]
    ones_bd = ones_ref[...]
    z = w0_ref[...] + _dot(jnp.tanh(xwa).astype(BF16), wup_ref[...])
    decay = jnp.exp(-DECAY_SCALE * jax.nn.sigmoid(z))
    a = jax.nn.sigmoid(a0_ref[...] + _dot(xwa.astype(BF16), aup_ref[...]))
    g = _dot(jax.nn.sigmoid(xg).astype(BF16), gup_ref[...])
    kk = k * kk_ref[...]
    kk = kk * lax.rsqrt(jnp.maximum(_head_sum(kk * kk, ones_bd), 1e-24))
    k_h = k * (1.0 + (a - 1.0) * ka_ref[...])
    g_o[...] = g
    bonus_o[...] = _head_sum(r * k_h * rk_ref[...], ones_bd) * v
    n_h = c_a // HEAD
    for val, dst_ref in ((r, r_o), (decay, w_o), (k_h, k_o), (v, v_o), (-kk, a_o), (kk * a, b_o)):
        _to_chain_minor(val, dst_ref, bg, n_h)


def _in_proj(x, g, w_bf16, widths, shift0, prm, bg, tm):
    rows, d = x.shape
    c_a = prm["w0"].shape[1]
    bounds, lo = [], 0
    for wd in widths:
        bounds.append((lo, lo + wd))
        lo += wd
    consts = [prm["mu"], prm["w0"], prm["a0"], prm["k_k"], prm["k_a"], prm["r_k"], prm["w_up"], prm["a_up"],
              prm["g_up"], prm["ones_bd"]]
    out_w = list(widths[1:]) + [c_a] * 2
    n_chain = _chain_geometry(bg, c_a // HEAD)[3] * LANES
    outs = pl.pallas_call(
        functools.partial(_in_proj_kernel, bg=bg, c_a=c_a, bounds=tuple(bounds)),
        grid=(rows // tm,),
        in_specs=[pl.BlockSpec((tm, d), lambda i: (i, 0)), _const_spec(g.shape),
                  pl.BlockSpec(w_bf16.shape, lambda i: (0, 0), pipeline_mode=pl.Buffered(1)),
                  _const_spec(shift0.shape)] + [_const_spec(c.shape) for c in consts],
        out_specs=[pl.BlockSpec((tm, wd), lambda i: (i, 0)) for wd in out_w]
                  + [pl.BlockSpec((tm // bg, HEAD, n_chain), lambda i: (i, 0, 0))] * 6 + [_const_spec(shift0.shape)],
        out_shape=[jax.ShapeDtypeStruct((rows, wd), F32) for wd in out_w]
                  + [jax.ShapeDtypeStruct((rows // bg, HEAD, n_chain), F32)] * 6
                  + [jax.ShapeDtypeStruct(shift0.shape, F32)],
        scratch_shapes=[pltpu.VMEM(shift0.shape, F32)],
        compiler_params=_params("arbitrary"),
        name="in_proj",
    )(x, g, w_bf16, shift0, *consts)
    return outs[0], outs[1], outs[2], outs[3], outs[4], outs[5:11], outs[11]


def _wkv_kernel(zr, zw, zk, zv, za, zb, s0_ref, zy_ref, st_ref, *, n_i, tc):
    dup = HEAD // n_i
    low_i = lax.broadcasted_iota(jnp.int32, (n_i, LANES), 1) < HEAD

    def row(ref, t, j):
        return ref[t, pl.ds(j, 1), :]

    @pl.when(pl.program_id(1) == 0)
    def _():
        st_ref[...] = s0_ref[...]

    sa0 = jnp.zeros((n_i, LANES), F32)
    for j in range(HEAD):
        sa0 = sa0 + st_ref[j] * row(za, 0, j)

    def step(t, sa):
        t_next = jnp.minimum(t + 1, tc - 1)
        v = zv[t] if dup == 1 else jnp.where(low_i, zv[t, :n_i, :], zv[t, n_i:, :])
        y = jnp.zeros((n_i, LANES), F32)
        sa_next = jnp.zeros((n_i, LANES), F32)
        for j in range(HEAD):
            s = st_ref[j] * row(zw, t, j) + sa * row(zb, t, j) + v * row(zk, t, j)
            st_ref[j] = s
            y = y + s * row(zr, t, j)
            sa_next = sa_next + s * row(za, t_next, j)
        zy_ref[t] = y
        return sa_next

    lax.fori_loop(0, tc, step, sa0)


def _wkv_group(operands, wkv0, bg, n_t):
    n_h = wkv0.shape[1]
    dup, nb, n_i, n_blk = _chain_geometry(bg, n_h)
    tc = min(n_t, 64)
    assert n_t % tc == 0
    s0 = wkv0.reshape(n_blk, nb, n_h, dup, n_i, HEAD).transpose(5, 4, 0, 3, 2, 1).reshape(HEAD, n_i, n_blk * LANES)
    seq = pl.BlockSpec((tc, HEAD, LANES), lambda c, t: (t, 0, c))
    state = pl.BlockSpec((HEAD, n_i, LANES), lambda c, t: (0, 0, c))
    return pl.pallas_call(
        functools.partial(_wkv_kernel, n_i=n_i, tc=tc),
        grid=(n_blk, n_t // tc),
        in_specs=[seq] * 6 + [state],
        out_specs=[pl.BlockSpec((tc, n_i, LANES), lambda c, t: (t, 0, c)), state],
        out_shape=[jax.ShapeDtypeStruct((n_t, n_i, n_blk * LANES), F32), jax.ShapeDtypeStruct(s0.shape, F32)],
        compiler_params=_params("arbitrary", "arbitrary"),
        name="wkv",
    )(*operands, s0)


def _wkv_states_from_chains(st, bg, n_h):
    depth, _, n_i, n_c = st.shape
    dup = HEAD // n_i
    nb = LANES // (n_h * dup)
    st = st.reshape(depth, HEAD, n_i, n_c // LANES, dup, n_h, nb).transpose(0, 3, 6, 5, 4, 2, 1)
    return st.reshape(depth, bg, n_h, HEAD, HEAD)


def _conf_kernel(p_ref, st0_ref, dww_ref, dwb_ref, lng_ref, lnb_ref, o_ref, st_o_ref, full_ref, *, bg, n_tap,
                 carry):
    tm, c2 = p_ref.shape
    c_b = c2 // 2
    halo = (n_tap - 1) * bg

    @pl.when(pl.program_id(0) == 0)
    def _():
        full_ref[0:halo, :] = st0_ref[...]

    p = p_ref[...]
    full_ref[halo:halo + tm, :] = p[:, :c_b] * jax.nn.sigmoid(p[:, c_b:])
    acc = jnp.zeros((tm, c_b), F32) + dwb_ref[...]
    for tap in range(n_tap):
        acc = acc + dww_ref[pl.ds(tap, 1), :] * full_ref[tap * bg:tap * bg + tm, :]
    mean = jnp.mean(acc, axis=-1, keepdims=True)
    cen = acc - mean
    var = jnp.mean(cen * cen, axis=-1, keepdims=True)
    n = cen * lax.rsqrt(var + LN_EPS) * lng_ref[...] + lnb_ref[...]
    o_ref[...] = (n * jax.nn.sigmoid(n)).astype(o_ref.dtype)
    tail = full_ref[tm:tm + halo, :]
    st_o_ref[...] = tail
    if carry:
        full_ref[0:halo, :] = tail


def _conformer(p_c, st0, prm, bg, tm):
    rows, c2 = p_c.shape
    c_b = c2 // 2
    n_tap = prm["dw_w"].shape[0]
    halo = (n_tap - 1) * bg
    consts = [prm["dw_w"], prm["dw_b"], prm["ln_g"], prm["ln_b"]]
    return pl.pallas_call(
        functools.partial(_conf_kernel, bg=bg, n_tap=n_tap, carry=rows > tm),
        grid=(rows // tm,),
        in_specs=[pl.BlockSpec((tm, c2), lambda i: (i, 0)), _const_spec(st0.shape)]
                 + [_const_spec(c.shape) for c in consts],
        out_specs=[pl.BlockSpec((tm, c_b), lambda i: (i, 0)), _const_spec((halo, c_b))],
        out_shape=[jax.ShapeDtypeStruct((rows, c_b), BF16), jax.ShapeDtypeStruct((halo, c_b), F32)],
        scratch_shapes=[pltpu.VMEM((halo + tm, c_b), F32)],
        compiler_params=_params("arbitrary"),
        name="conformer",
    )(p_c, st0, *consts)


S5_LANE_CHUNK = 512


def _s5_kernel(u_ref, h0r_ref, h0i_ref, lamr_ref, lami_ref, dt_ref, bre_ref, bim_ref, cre_ref, cim_ref, d_ref,
               wglu_ref, o_ref, hr_o, hi_o, sre_ref, sim_ref, coef_ref, hr_s, hi_s, *, bg):
    tm = u_ref.shape[0]
    n_state = sre_ref.shape[1]
    d_model = o_ref.shape[1]

    @pl.when(pl.program_id(0) == 0)
    def _():
        lr = lamr_ref[...]
        li = lami_ref[...]
        dt = jnp.exp(dt_ref[...])
        mag = jnp.exp(lr * dt)
        ab_re = mag * jnp.cos(li * dt)
        ab_im = mag * jnp.sin(li * dt)
        den = lr * lr + li * li
        coef_ref[0:1, :] = ab_re
        coef_ref[1:2, :] = ab_im
        coef_ref[2:3, :] = ((ab_re - 1.0) * lr + ab_im * li) / den
        coef_ref[3:4, :] = (ab_im * lr - (ab_re - 1.0) * li) / den
        hr_s[...] = h0r_ref[...]
        hi_s[...] = h0i_ref[...]

    u = u_ref[...]
    ub = u.astype(BF16)
    n_slab = bre_ref.shape[0]
    w_state = n_state // n_slab
    for m in range(n_slab):
        um = ub[:, m * LANES:(m + 1) * LANES]
        st = slice(m * w_state, (m + 1) * w_state)
        bu_re = _dot(um, bre_ref[m])
        bu_im = _dot(um, bim_ref[m])
        q_re = coef_ref[2:3, st]
        q_im = coef_ref[3:4, st]
        sre_ref[:, st] = q_re * bu_re - q_im * bu_im
        sim_ref[:, st] = q_re * bu_im + q_im * bu_re

    n_step = tm // bg
    for lc in range(n_state // S5_LANE_CHUNK):
        lanes = slice(lc * S5_LANE_CHUNK, (lc + 1) * S5_LANE_CHUNK)
        ab_re = jnp.broadcast_to(coef_ref[0:1, lanes], (SUBLANES, S5_LANE_CHUNK))
        ab_im = jnp.broadcast_to(coef_ref[1:2, lanes], (SUBLANES, S5_LANE_CHUNK))

        def strip(rb, _):
            r0 = pl.multiple_of(rb * SUBLANES, SUBLANES)

            def step(t, carry):
                s_re, s_im = carry
                rows = pl.ds(pl.multiple_of(t * bg + r0, SUBLANES), SUBLANES)
                n_re = ab_re * s_re - ab_im * s_im + sre_ref[rows, lanes]
                n_im = ab_re * s_im + ab_im * s_re + sim_ref[rows, lanes]
                sre_ref[rows, lanes] = n_re
                sim_ref[rows, lanes] = n_im
                return n_re, n_im

            rs = pl.ds(r0, SUBLANES)
            s_re, s_im = lax.fori_loop(0, n_step, step, (hr_s[rs, lanes], hi_s[rs, lanes]))
            hr_s[rs, lanes] = s_re
            hi_s[rs, lanes] = s_im
            return 0

        lax.fori_loop(0, bg // SUBLANES, strip, 0)

    hr_o[...] = hr_s[...]
    hi_o[...] = hi_s[...]
    y = jnp.concatenate(
        [_dot(sre_ref[:, m * w_state:(m + 1) * w_state].astype(BF16), cre_ref[m])
         - _dot(sim_ref[:, m * w_state:(m + 1) * w_state].astype(BF16), cim_ref[m]) for m in range(n_slab)], axis=1)
    y = y + d_ref[...] * u
    z = _dot(y.astype(BF16), wglu_ref[...])
    o_ref[...] = z[:, :d_model] * jax.nn.sigmoid(z[:, d_model:])


def _s5(u, h0r, h0i, prm, bg, tm):
    rows, c_c = u.shape
    n_state = h0r.shape[1]
    d_model = prm["w_glu"].shape[1] // 2
    consts = [prm["lam_re"], prm["lam_im"], prm["log_dt"], prm["b_re"], prm["b_im"], prm["c_re"], prm["c_im"],
              prm["d"], prm["w_glu"]]
    return pl.pallas_call(
        functools.partial(_s5_kernel, bg=bg),
        grid=(rows // tm,),
        in_specs=[pl.BlockSpec((tm, c_c), lambda i: (i, 0)), _const_spec(h0r.shape), _const_spec(h0i.shape)]
                 + [_const_spec(c.shape) for c in consts],
        out_specs=[pl.BlockSpec((tm, d_model), lambda i: (i, 0)), _const_spec(h0r.shape), _const_spec(h0i.shape)],
        out_shape=[jax.ShapeDtypeStruct((rows, d_model), F32), jax.ShapeDtypeStruct(h0r.shape, F32),
                   jax.ShapeDtypeStruct(h0i.shape, F32)],
        scratch_shapes=[pltpu.VMEM((tm, n_state), F32), pltpu.VMEM((tm, n_state), F32),
                        pltpu.VMEM((SUBLANES, n_state), F32), pltpu.VMEM(h0r.shape, F32),
                        pltpu.VMEM(h0i.shape, F32)],
        compiler_params=_params("arbitrary"),
        name="s5",
    )(u, h0r, h0i, *consts)


def _merge_kernel(zy_ref, bonus_ref, g_ref, yb_ref, yc_ref, pg_ref, x_ref, lng_ref, lnb_ref, ones_ref, woa_ref,
                  wob_ref, bgate_ref, wout_ref, o_ref, *, bg):
    d_model = x_ref.shape[1]
    ones_bd = ones_ref[...]
    y = _from_chain_minor(zy_ref, bg, bonus_ref.shape[1] // HEAD)
    cen = y - _head_sum(y, ones_bd) * (1.0 / HEAD)
    var = _head_sum(cen * cen, ones_bd) * (1.0 / HEAD)
    yn = cen * lax.rsqrt(var + GN_EPS) * lng_ref[...] + lnb_ref[...]
    y_a = _dot(((yn + bonus_ref[...]) * g_ref[...]).astype(BF16), woa_ref[...])
    y_b = _dot(yb_ref[...], wob_ref[...])
    gates = jax.nn.sigmoid(pg_ref[...] + bgate_ref[...])
    h = (gates[:, :d_model] * y_a + gates[:, d_model:2 * d_model] * y_b + gates[:, 2 * d_model:] * yc_ref[...])
    o_ref[...] = x_ref[...] + _dot(h.astype(BF16), wout_ref[...])


def _merge(zy, bonus, g, yb, yc, pg, x, prm, bg, tm):
    rows, d_model = x.shape
    acts = [bonus, g, yb, yc, pg, x]
    consts = [prm["ln_g"], prm["ln_b"], prm["ones_bd"], prm["rwkv_w_out"], prm["conf_w_out"], prm["b_gate"],
              prm["w_out"]]
    return pl.pallas_call(
        functools.partial(_merge_kernel, bg=bg),
        grid=(rows // tm,),
        in_specs=[pl.BlockSpec((tm // bg,) + zy.shape[1:], lambda i: (i, 0, 0))]
                 + [pl.BlockSpec((tm, a.shape[1]), lambda i: (i, 0)) for a in acts]
                 + [_const_spec(c.shape) for c in consts],
        out_specs=pl.BlockSpec((tm, d_model), lambda i: (i, 0)),
        out_shape=jax.ShapeDtypeStruct((rows, d_model), F32),
        compiler_params=_params("arbitrary"),
        name="merge",
    )(zy, *acts, *consts)


def _ffn_kernel(x_ref, st_ref, g_ref, wu_ref, dw_ref, db_ref, wd_ref, gf_ref, o_ref, carry_ref, *full_refs, bg, n_tap,
                d_ff, final_norm):
    tm = x_ref.shape[0]
    halo = (n_tap - 1) * bg

    @pl.when(pl.program_id(0) == 0)
    def _():
        carry_ref[...] = st_ref[...]

    x = x_ref[...]
    xn = _rms_scale(x, g_ref[...]).astype(BF16)
    acts = []
    for ci, lo in enumerate(range(0, d_ff, MXU_DIM)):
        wd = min(MXU_DIM, d_ff - lo)
        halves = []
        for h in range(2):
            cols = slice(h * d_ff + lo, h * d_ff + lo + wd)
            full_ref = full_refs[(2 * ci + h) % len(full_refs)]
            full_ref[0:halo, 0:wd] = carry_ref[:, cols]
            full_ref[halo:halo + tm, 0:wd] = _dot(xn, wu_ref[:, cols])
            c = jnp.zeros((tm, wd), F32) + db_ref[:, cols]
            for tap in range(n_tap):
                c = c + dw_ref[pl.ds(tap, 1), cols] * full_ref[tap * bg:tap * bg + tm, 0:wd]
            carry_ref[:, cols] = full_ref[tm:tm + halo, 0:wd]
            halves.append(c)
        acts.append((halves[0] * jax.nn.sigmoid(halves[0]) * halves[1]).astype(BF16))
    out = x + _dot(jnp.concatenate(acts, axis=1), wd_ref[...])
    o_ref[...] = _rms_scale(out, gf_ref[...]) if final_norm else out


FFN_STAGING_BUFFERS = 4


def _conv_ffn(x, st0, prm, final_g, bg, tm, final_norm):
    rows, d_model = x.shape
    d_ff = prm["w_down"].shape[0]
    n_tap = prm["dw_w"].shape[0]
    halo = (n_tap - 1) * bg
    resident = lambda a: pl.BlockSpec(a.shape, lambda i: (0,) * a.ndim, pipeline_mode=pl.Buffered(1))
    return pl.pallas_call(
        functools.partial(_ffn_kernel, bg=bg, n_tap=n_tap, d_ff=d_ff, final_norm=final_norm),
        grid=(rows // tm,),
        in_specs=[pl.BlockSpec((tm, d_model), lambda i: (i, 0)), resident(st0), _const_spec(prm["norm_g"].shape),
                  resident(prm["w_up"]), _const_spec(prm["dw_w"].shape), _const_spec(prm["dw_b"].shape),
                  resident(prm["w_down"]), _const_spec(final_g.shape)],
        out_specs=[pl.BlockSpec((tm, d_model), lambda i: (i, 0)), _const_spec(st0.shape)],
        out_shape=[jax.ShapeDtypeStruct((rows, d_model), F32), jax.ShapeDtypeStruct(st0.shape, F32)],
        scratch_shapes=[pltpu.VMEM((halo + tm, MXU_DIM), F32)] * FFN_STAGING_BUFFERS,
        compiler_params=_params("arbitrary"),
        name="conv_ffn",
    )(x, st0, prm["norm_g"], prm["w_up"], prm["dw_w"], prm["dw_b"], prm["w_down"], final_g)


def _tile(rows, bg, pref):
    tm = max(bg, min(pref, rows))
    while rows % tm or tm % bg:
        tm -= bg
    return tm


def _time_major(st):
    return st.transpose(1, 0, 2).reshape(-1, st.shape[-1])


def _run_trunk(x_bm, states, layers, final_g):
    bg, n_t, d_model = x_bm.shape
    rows = bg * n_t
    x = x_bm.transpose(1, 0, 2).reshape(rows, d_model)
    shift, wkv, conv, ssm_re, ssm_im, ffn_conv = states
    outs = ([], [], [], [], [], [])
    for l, prm in enumerate(layers):
        p_c, p_s, p_g, g, bonus, wkv_operands, shift_n = _in_proj(
            x, prm["norm_mix_g"], prm["w_in"], prm["in_widths"], shift[l], prm["rwkv"], bg,
            _tile(rows, 2 * bg, 256))
        zy, wkv_n = _wkv_group(wkv_operands, wkv[l], bg, n_t)
        yb, conv_n = _conformer(p_c, _time_major(conv[l]), prm["conf"], bg, _tile(rows, bg, 1024 if bg > 8 else 512))
        n_state = ssm_re.shape[2] * ssm_re.shape[3]
        yc, re_n, im_n = _s5(p_s, ssm_re[l].reshape(bg, n_state), ssm_im[l].reshape(bg, n_state), prm["s5"], bg,
                             _tile(rows, bg, 256 if bg > 8 else 512))
        x = _merge(zy, bonus, g, yb, yc, p_g, x, prm["merge"], bg, _tile(rows, 2 * bg, 512))
        x, ffn_n = _conv_ffn(x, _time_major(ffn_conv[l]), prm["ffn"], final_g, bg, _tile(rows, bg, 512),
                             final_norm=(l == len(layers) - 1))
        for lst, s in zip(outs, (shift_n, wkv_n, conv_n, re_n, im_n, ffn_n)):
            lst.append(s)
    y = x.reshape(n_t, bg, d_model).transpose(1, 0, 2)
    shift_n, wkv_n, conv_n, re_n, im_n, ffn_n = (jnp.stack(lst) for lst in outs)
    depth = len(layers)
    new_states = [shift_n, _wkv_states_from_chains(wkv_n, bg, wkv.shape[2]),
                  conv_n.reshape(depth, -1, bg, conv_n.shape[-1]).transpose(0, 2, 1, 3),
                  re_n.reshape(ssm_re.shape), im_n.reshape(ssm_im.shape),
                  ffn_n.reshape(depth, -1, bg, ffn_n.shape[-1]).transpose(0, 2, 1, 3)]
    return y, new_states


def _block_diag(blocks):
    n_g, a, b = blocks.shape
    eye = jnp.eye(n_g, dtype=blocks.dtype)
    return (eye[:, None, :, None] * blocks[:, :, None, :]).reshape(n_g * a, n_g * b)


def _prep_layer(l, w):
    row = lambda v: v.reshape(1, -1).astype(F32)
    c_a = w["rwkv_w0"].shape[1]
    n_rwkv = w["rwkv_mu"].shape[1]
    c_b = w["conf_dw_b"].shape[1]
    c_c = w["s5_d"].shape[1]
    d_model = w["w_out"].shape[1]
    lora_w = w["rwkv_w_up"].shape[1]
    lora_a = w["rwkv_a_up"].shape[1]
    assert lora_w + lora_a == LORA_PAIR and n_rwkv == 3 * c_a + LORA_PAIR + w["rwkv_g_up"].shape[1]
    n_grp, p_c = w["s5_lambda_re"].shape[1:]
    grp_per_slab = LANES // (c_c // n_grp)

    def slabs(blocks):
        t = blocks.transpose(0, 2, 1)
        t = t.reshape(n_grp // grp_per_slab, grp_per_slab, *t.shape[1:])
        return jax.vmap(_block_diag)(t).astype(BF16)

    assert c_a % MXU_DIM == 0
    ones_bd = _block_diag(jnp.ones((MXU_DIM // HEAD, HEAD, HEAD), F32)).astype(BF16)
    rwkv = dict(
        mu=row(w["rwkv_mu"][l]), w0=row(w["rwkv_w0"][l]), a0=row(w["rwkv_a0"][l]), k_k=row(w["rwkv_k_k"][l]),
        k_a=row(w["rwkv_k_a"][l]), r_k=row(w["rwkv_r_k"][l]),
        w_up=jnp.concatenate([w["rwkv_w_up"][l], jnp.zeros((lora_a, c_a), F32)], axis=0).astype(BF16),
        a_up=jnp.concatenate([jnp.zeros((lora_w, c_a), F32), w["rwkv_a_up"][l]], axis=0).astype(BF16),
        g_up=w["rwkv_g_up"][l].astype(BF16), ones_bd=ones_bd)
    conf = dict(dw_w=w["conf_dw_w"][l], dw_b=row(w["conf_dw_b"][l]), ln_g=row(w["conf_ln_g"][l]),
                ln_b=row(w["conf_ln_b"][l]))
    s5 = dict(
        lam_re=row(w["s5_lambda_re"][l]), lam_im=row(w["s5_lambda_im"][l]),
        log_dt=row(jnp.repeat(w["s5_log_dt"][l], p_c)),
        b_re=slabs(w["s5_b_re"][l]), b_im=slabs(w["s5_b_im"][l]),
        c_re=slabs(w["s5_c_re"][l]), c_im=slabs(w["s5_c_im"][l]),
        d=row(w["s5_d"][l]), w_glu=w["s5_w_glu"][l].astype(BF16))
    merge = dict(ln_g=row(w["rwkv_ln_g"][l]), ln_b=row(w["rwkv_ln_b"][l]), ones_bd=ones_bd,
                 rwkv_w_out=w["rwkv_w_out"][l].astype(BF16), conf_w_out=w["conf_w_out"][l].astype(BF16),
                 b_gate=row(w["b_gate"][l]), w_out=w["w_out"][l].astype(BF16))
    ffn = dict(norm_g=row(w["norm_ffn_g"][l]), w_up=w["ffn_w_up"][l].astype(BF16), dw_w=w["ffn_dw_w"][l],
               dw_b=row(w["ffn_dw_b"][l]), w_down=w["ffn_w_down"][l].astype(BF16))
    in_widths = (n_rwkv, 2 * c_b, c_c, w["b_gate"].shape[1])
    assert sum(in_widths) == w["w_in"].shape[2] and d_model * 3 == in_widths[3] and n_grp * p_c > 0
    return dict(norm_mix_g=row(w["norm_mix_g"][l]), w_in=w["w_in"][l].astype(BF16), in_widths=in_widths,
                rwkv=rwkv, conf=conf, s5=s5, merge=merge, ffn=ffn)


def kernel(x_prompt, x_sample, state_shift, state_wkv, state_conv, state_ssm_re, state_ssm_im, state_ffn_conv, norm_mix_g, w_in, b_gate, rwkv_mu, rwkv_w0, rwkv_w_up, rwkv_a0, rwkv_a_up, rwkv_g_up, rwkv_k_k, rwkv_k_a, rwkv_r_k, rwkv_ln_g, rwkv_ln_b, rwkv_w_out, conf_dw_w, conf_dw_b, conf_ln_g, conf_ln_b, conf_w_out, s5_lambda_re, s5_lambda_im, s5_log_dt, s5_b_re, s5_b_im, s5_c_re, s5_c_im, s5_d, s5_w_glu, w_out, norm_ffn_g, ffn_w_up, ffn_dw_w, ffn_dw_b, ffn_w_down, norm_final_g):
    w = dict(norm_mix_g=norm_mix_g, w_in=w_in, b_gate=b_gate, rwkv_mu=rwkv_mu, rwkv_w0=rwkv_w0,
             rwkv_w_up=rwkv_w_up, rwkv_a0=rwkv_a0, rwkv_a_up=rwkv_a_up, rwkv_g_up=rwkv_g_up, rwkv_k_k=rwkv_k_k,
             rwkv_k_a=rwkv_k_a, rwkv_r_k=rwkv_r_k, rwkv_ln_g=rwkv_ln_g, rwkv_ln_b=rwkv_ln_b,
             rwkv_w_out=rwkv_w_out, conf_dw_w=conf_dw_w, conf_dw_b=conf_dw_b, conf_ln_g=conf_ln_g,
             conf_ln_b=conf_ln_b, conf_w_out=conf_w_out, s5_lambda_re=s5_lambda_re, s5_lambda_im=s5_lambda_im,
             s5_log_dt=s5_log_dt, s5_b_re=s5_b_re, s5_b_im=s5_b_im, s5_c_re=s5_c_re, s5_c_im=s5_c_im, s5_d=s5_d,
             s5_w_glu=s5_w_glu, w_out=w_out, norm_ffn_g=norm_ffn_g, ffn_w_up=ffn_w_up, ffn_dw_w=ffn_dw_w,
             ffn_dw_b=ffn_dw_b, ffn_w_down=ffn_w_down)
    depth = w_in.shape[0]
    layers = [_prep_layer(l, w) for l in range(depth)]
    final_g = norm_final_g.reshape(1, -1)
    bp = x_prompt.shape[0]
    zeros_like_state = lambda s: jnp.zeros((depth, bp) + s.shape[2:], s.dtype)
    prompt_states = tuple(zeros_like_state(s) for s in
                          (state_shift, state_wkv, state_conv, state_ssm_re, state_ssm_im, state_ffn_conv))
    y_prompt, p_states = _run_trunk(x_prompt, prompt_states, layers, final_g)
    y_sample, s_states = _run_trunk(
        x_sample, (state_shift, state_wkv, state_conv, state_ssm_re, state_ssm_im, state_ffn_conv), layers, final_g)
    return (y_prompt, y_sample, *p_states, *s_states)
```

```python
import functools
import math

import jax
import jax.numpy as jnp
from jax import lax
from jax.experimental import pallas as pl
from jax.experimental.pallas import tpu as pltpu

F32 = jnp.float32
BF16 = jnp.bfloat16

VMEM_LIMIT_BYTES = 56 * 1024 * 1024
SUBLANES = 8
LANES = 128
MXU_DIM = 256

HEAD = 64
LORA_PAIR = 128
RMS_EPS = 1e-6
LN_EPS = 1e-5
GN_EPS = 64e-5
DECAY_SCALE = math.exp(-0.5)


def _params(*sem):
    return pltpu.CompilerParams(dimension_semantics=sem, vmem_limit_bytes=VMEM_LIMIT_BYTES)


def _const_spec(shape):
    zeros = (0,) * len(shape)
    return pl.BlockSpec(shape, lambda *_: zeros)


def _dot(a, b):
    return jnp.dot(a, b, preferred_element_type=F32)


def _head_sum(x, ones_bd):
    hi = x.astype(BF16)
    lo = (x - hi.astype(F32)).astype(BF16)
    wd = ones_bd.shape[0]
    return jnp.concatenate(
        [_dot(hi[:, s:s + wd], ones_bd) + _dot(lo[:, s:s + wd], ones_bd) for s in range(0, x.shape[1], wd)], axis=1)


def _rms_scale(x, g):
    ms = jnp.mean(x * x, axis=-1, keepdims=True)
    return x * lax.rsqrt(ms + RMS_EPS) * g


def _chain_geometry(bg, n_h):
    dup = max(1, LANES // (bg * n_h))
    nb = LANES // (n_h * dup)
    assert n_h % 2 == 0 and nb % SUBLANES == 0 and bg % nb == 0
    return dup, nb, HEAD // dup, bg // nb


def _swap_halves(x):
    return pltpu.roll(x, HEAD, 1)


def _to_chain_minor(x, dst_ref, bg, n_h):
    dup, nb, _, n_blk = _chain_geometry(bg, n_h)
    low = lax.broadcasted_iota(jnp.int32, (nb, LANES), 1) < HEAD
    for t0 in range(0, x.shape[0] // bg, 2):
        for c in range(n_blk):
            r0 = t0 * bg + c * nb
            r1 = r0 + bg
            pieces = []
            for q in range(n_h // 2):
                x0 = x[r0:r0 + nb, q * LANES:(q + 1) * LANES]
                x1 = x[r1:r1 + nb, q * LANES:(q + 1) * LANES]
                pieces.append(jnp.where(low, x0, _swap_halves(x1)))
                pieces.append(jnp.where(low, _swap_halves(x0), x1))
            tt = jnp.concatenate(pieces * dup, axis=0).T
            dst_ref[t0, :, c * LANES:(c + 1) * LANES] = tt[:HEAD]
            dst_ref[t0 + 1, :, c * LANES:(c + 1) * LANES] = tt[HEAD:]


def _from_chain_minor(src_ref, bg, n_h):
    dup, nb, _, n_blk = _chain_geometry(bg, n_h)
    n_step = src_ref.shape[0]
    low = lax.broadcasted_iota(jnp.int32, (nb, LANES), 1) < HEAD
    tiles = {}
    for t0 in range(0, n_step, 2):
        for c in range(n_blk):
            ys = []
            for t in (t0, t0 + 1):
                y = src_ref[t, :, c * LANES:(c + 1) * LANES]
                ys += [y] if dup == 1 else [y, _swap_halves(y)]
            wt = jnp.concatenate(ys, axis=0).T
            first, second = [], []
            for q in range(n_h // 2):
                even = wt[2 * q * nb:(2 * q + 1) * nb]
                odd = wt[(2 * q + 1) * nb:(2 * q + 2) * nb]
                first.append(jnp.where(low, even, _swap_halves(odd)))
                second.append(jnp.where(low, _swap_halves(even), odd))
            tiles[t0, c] = jnp.concatenate(first, axis=1)
            tiles[t0 + 1, c] = jnp.concatenate(second, axis=1)
    return jnp.concatenate([tiles[t, c] for t in range(n_step) for c in range(n_blk)], axis=0)


def _in_proj_kernel(x_ref, g_ref, w_ref, s0_ref, mu_ref, w0_ref, a0_ref, kk_ref, ka_ref, rk_ref, wup_ref, aup_ref,
                    gup_ref, ones_ref, pc_o, ps_o, pg_o, g_o, bonus_o, r_o, w_o, k_o, v_o, a_o, b_o, shift_o,
                    carry_ref, *, bg, c_a, bounds):
    @pl.when(pl.program_id(0) == 0)
    def _():
        carry_ref[...] = s0_ref[...]

    xn = _rms_scale(x_ref[...], g_ref[...]).astype(BF16)
    p = _dot(xn, w_ref[:, bounds[0][0]:bounds[0][1]])
    tm = p.shape[0]
    prev = carry_ref[...] if tm == bg else jnp.concatenate([carry_ref[...], p[:tm - bg]], axis=0)
    carry_ref[...] = p[tm - bg:]
    shift_o[...] = p[tm - bg:]
    xm = p + (prev - p) * mu_ref[...]
    r = xm[:, 0:c_a]
    k = xm[:, c_a:2 * c_a]
    v = xm[:, 2 * c_a:3 * c_a]
    xwa = xm[:, 3 * c_a:3 * c_a + LORA_PAIR]
    xg = xm[:, 3 * c_a + LORA_PAIR:]
    ones_bd = ones_ref[...]
    z = w0_ref[...] + _dot(jnp.tanh(xwa).astype(BF16), wup_ref[...])
    decay = jnp.exp(-DECAY_SCALE * jax.nn.sigmoid(z))
    a = jax.nn.sigmoid(a0_ref[...] + _dot(xwa.astype(BF16), aup_ref[...]))
    g = _dot(jax.nn.sigmoid(xg).astype(BF16), gup_ref[...])
    kk = k * kk_ref[...]
    kk = kk * lax.rsqrt(jnp.maximum(_head_sum(kk * kk, ones_bd), 1e-24))
    k_h = k * (1.0 + (a - 1.0) * ka_ref[...])
    g_o[...] = g
    bonus_o[...] = _head_sum(r * k_h * rk_ref[...], ones_bd) * v
    n_h = c_a // HEAD
    for val, dst_ref in ((r, r_o), (decay, w_o), (k_h, k_o), (v, v_o), (-kk, a_o), (kk * a, b_o)):
        _to_chain_minor(val, dst_ref, bg, n_h)
    for o_ref, (lo, hi) in zip((pc_o, ps_o, pg_o), bounds[1:]):
        o_ref[...] = _dot(xn, w_ref[:, lo:hi])


def _in_proj(x, g, w_bf16, widths, shift0, prm, bg, tm):
    rows, d = x.shape
    c_a = prm["w0"].shape[1]
    bounds, lo = [], 0
    for wd in widths:
        bounds.append((lo, lo + wd))
        lo += wd
    consts = [prm["mu"], prm["w0"], prm["a0"], prm["k_k"], prm["k_a"], prm["r_k"], prm["w_up"], prm["a_up"],
              prm["g_up"], prm["ones_bd"]]
    out_w = list(widths[1:]) + [c_a] * 2
    n_chain = _chain_geometry(bg, c_a // HEAD)[3] * LANES
    outs = pl.pallas_call(
        functools.partial(_in_proj_kernel, bg=bg, c_a=c_a, bounds=tuple(bounds)),
        grid=(rows // tm,),
        in_specs=[pl.BlockSpec((tm, d), lambda i: (i, 0)), _const_spec(g.shape),
                  pl.BlockSpec(w_bf16.shape, lambda i: (0, 0), pipeline_mode=pl.Buffered(1)),
                  _const_spec(shift0.shape)] + [_const_spec(c.shape) for c in consts],
        out_specs=[pl.BlockSpec((tm, wd), lambda i: (i, 0)) for wd in out_w]
                  + [pl.BlockSpec((tm // bg, HEAD, n_chain), lambda i: (i, 0, 0))] * 6 + [_const_spec(shift0.shape)],
        out_shape=[jax.ShapeDtypeStruct((rows, wd), F32) for wd in out_w]
                  + [jax.ShapeDtypeStruct((rows // bg, HEAD, n_chain), F32)] * 6
                  + [jax.ShapeDtypeStruct(shift0.shape, F32)],
        scratch_shapes=[pltpu.VMEM(shift0.shape, F32)],
        compiler_params=_params("arbitrary"),
        name="in_proj",
    )(x, g, w_bf16, shift0, *consts)
    return outs[0], outs[1], outs[2], outs[3], outs[4], outs[5:11], outs[11]


def _wkv_kernel(zr, zw, zk, zv, za, zb, s0_ref, zy_ref, st_ref, *, n_i, tc):
    dup = HEAD // n_i
    low_i = lax.broadcasted_iota(jnp.int32, (n_i, LANES), 1) < HEAD

    def row(ref, t, j):
        return ref[t, pl.ds(j, 1), :]

    @pl.when(pl.program_id(1) == 0)
    def _():
        st_ref[...] = s0_ref[...]

    sa0 = jnp.zeros((n_i, LANES), F32)
    for j in range(HEAD):
        sa0 = sa0 + st_ref[j] * row(za, 0, j)

    def step(t, sa):
        t_next = jnp.minimum(t + 1, tc - 1)
        v = zv[t] if dup == 1 else jnp.where(low_i, zv[t, :n_i, :], zv[t, n_i:, :])
        y = jnp.zeros((n_i, LANES), F32)
        sa_next = jnp.zeros((n_i, LANES), F32)
        for j in range(HEAD):
            s = st_ref[j] * row(zw, t, j) + sa * row(zb, t, j) + v * row(zk, t, j)
            st_ref[j] = s
            y = y + s * row(zr, t, j)
            sa_next = sa_next + s * row(za, t_next, j)
        zy_ref[t] = y
        return sa_next

    lax.fori_loop(0, tc, step, sa0)


def _wkv_group(operands, wkv0, bg, n_t):
    n_h = wkv0.shape[1]
    dup, nb, n_i, n_blk = _chain_geometry(bg, n_h)
    tc = min(n_t, 64)
    assert n_t % tc == 0
    s0 = wkv0.reshape(n_blk, nb, n_h, dup, n_i, HEAD).transpose(5, 4, 0, 3, 2, 1).reshape(HEAD, n_i, n_blk * LANES)
    seq = pl.BlockSpec((tc, HEAD, LANES), lambda c, t: (t, 0, c))
    state = pl.BlockSpec((HEAD, n_i, LANES), lambda c, t: (0, 0, c))
    return pl.pallas_call(
        functools.partial(_wkv_kernel, n_i=n_i, tc=tc),
        grid=(n_blk, n_t // tc),
        in_specs=[seq] * 6 + [state],
        out_specs=[pl.BlockSpec((tc, n_i, LANES), lambda c, t: (t, 0, c)), state],
        out_shape=[jax.ShapeDtypeStruct((n_t, n_i, n_blk * LANES), F32), jax.ShapeDtypeStruct(s0.shape, F32)],
        compiler_params=_params("arbitrary", "arbitrary"),
        name="wkv",
    )(*operands, s0)


def _wkv_states_from_chains(st, bg, n_h):
    depth, _, n_i, n_c = st.shape
    dup = HEAD // n_i
    nb = LANES // (n_h * dup)
    st = st.reshape(depth, HEAD, n_i, n_c // LANES, dup, n_h, nb).transpose(0, 3, 6, 5, 4, 2, 1)
    return st.reshape(depth, bg, n_h, HEAD, HEAD)


def _conf_kernel(p_ref, st0_ref, dww_ref, dwb_ref, lng_ref, lnb_ref, o_ref, st_o_ref, full_ref, *, bg, n_tap,
                 carry):
    tm, c2 = p_ref.shape
    c_b = c2 // 2
    halo = (n_tap - 1) * bg

    @pl.when(pl.program_id(0) == 0)
    def _():
        full_ref[0:halo, :] = st0_ref[...]

    p = p_ref[...]
    full_ref[halo:halo + tm, :] = p[:, :c_b] * jax.nn.sigmoid(p[:, c_b:])
    acc = jnp.zeros((tm, c_b), F32) + dwb_ref[...]
    for tap in range(n_tap):
        acc = acc + dww_ref[pl.ds(tap, 1), :] * full_ref[tap * bg:tap * bg + tm, :]
    mean = jnp.mean(acc, axis=-1, keepdims=True)
    cen = acc - mean
    var = jnp.mean(cen * cen, axis=-1, keepdims=True)
    n = cen * lax.rsqrt(var + LN_EPS) * lng_ref[...] + lnb_ref[...]
    o_ref[...] = (n * jax.nn.sigmoid(n)).astype(o_ref.dtype)
    tail = full_ref[tm:tm + halo, :]
    st_o_ref[...] = tail
    if carry:
        full_ref[0:halo, :] = tail


def _conformer(p_c, st0, prm, bg, tm):
    rows, c2 = p_c.shape
    c_b = c2 // 2
    n_tap = prm["dw_w"].shape[0]
    halo = (n_tap - 1) * bg
    consts = [prm["dw_w"], prm["dw_b"], prm["ln_g"], prm["ln_b"]]
    return pl.pallas_call(
        functools.partial(_conf_kernel, bg=bg, n_tap=n_tap, carry=rows > tm),
        grid=(rows // tm,),
        in_specs=[pl.BlockSpec((tm, c2), lambda i: (i, 0)), _const_spec(st0.shape)]
                 + [_const_spec(c.shape) for c in consts],
        out_specs=[pl.BlockSpec((tm, c_b), lambda i: (i, 0)), _const_spec((halo, c_b))],
        out_shape=[jax.ShapeDtypeStruct((rows, c_b), BF16), jax.ShapeDtypeStruct((halo, c_b), F32)],
        scratch_shapes=[pltpu.VMEM((halo + tm, c_b), F32)],
        compiler_params=_params("arbitrary"),
        name="conformer",
    )(p_c, st0, *consts)


S5_LANE_CHUNK = 512


def _s5_kernel(u_ref, h0r_ref, h0i_ref, lamr_ref, lami_ref, dt_ref, bre_ref, bim_ref, cre_ref, cim_ref, d_ref,
               wglu_ref, o_ref, hr_o, hi_o, sre_ref, sim_ref, coef_ref, hr_s, hi_s, *, bg):
    tm = u_ref.shape[0]
    n_state = sre_ref.shape[1]
    d_model = o_ref.shape[1]

    @pl.when(pl.program_id(0) == 0)
    def _():
        lr = lamr_ref[...]
        li = lami_ref[...]
        dt = jnp.exp(dt_ref[...])
        mag = jnp.exp(lr * dt)
        ab_re = mag * jnp.cos(li * dt)
        ab_im = mag * jnp.sin(li * dt)
        den = lr * lr + li * li
        coef_ref[0:1, :] = ab_re
        coef_ref[1:2, :] = ab_im
        coef_ref[2:3, :] = ((ab_re - 1.0) * lr + ab_im * li) / den
        coef_ref[3:4, :] = (ab_im * lr - (ab_re - 1.0) * li) / den
        hr_s[...] = h0r_ref[...]
        hi_s[...] = h0i_ref[...]

    u = u_ref[...]
    ub = u.astype(BF16)
    n_slab = bre_ref.shape[0]
    w_state = n_state // n_slab
    for m in range(n_slab):
        um = ub[:, m * LANES:(m + 1) * LANES]
        st = slice(m * w_state, (m + 1) * w_state)
        bu_re = _dot(um, bre_ref[m])
        bu_im = _dot(um, bim_ref[m])
        q_re = coef_ref[2:3, st]
        q_im = coef_ref[3:4, st]
        sre_ref[:, st] = q_re * bu_re - q_im * bu_im
        sim_ref[:, st] = q_re * bu_im + q_im * bu_re

    n_step = tm // bg
    for lc in range(n_state // S5_LANE_CHUNK):
        lanes = slice(lc * S5_LANE_CHUNK, (lc + 1) * S5_LANE_CHUNK)
        ab_re = jnp.broadcast_to(coef_ref[0:1, lanes], (SUBLANES, S5_LANE_CHUNK))
        ab_im = jnp.broadcast_to(coef_ref[1:2, lanes], (SUBLANES, S5_LANE_CHUNK))

        def strip(rb, _):
            r0 = pl.multiple_of(rb * SUBLANES, SUBLANES)

            def step(t, carry):
                s_re, s_im = carry
                rows = pl.ds(pl.multiple_of(t * bg + r0, SUBLANES), SUBLANES)
                n_re = ab_re * s_re - ab_im * s_im + sre_ref[rows, lanes]
                n_im = ab_re * s_im + ab_im * s_re + sim_ref[rows, lanes]
                sre_ref[rows, lanes] = n_re
                sim_ref[rows, lanes] = n_im
                return n_re, n_im

            rs = pl.ds(r0, SUBLANES)
            s_re, s_im = lax.fori_loop(0, n_step, step, (hr_s[rs, lanes], hi_s[rs, lanes]))
            hr_s[rs, lanes] = s_re
            hi_s[rs, lanes] = s_im
            return 0

        lax.fori_loop(0, bg // SUBLANES, strip, 0)

    hr_o[...] = hr_s[...]
    hi_o[...] = hi_s[...]
    y = jnp.concatenate(
        [_dot(sre_ref[:, m * w_state:(m + 1) * w_state].astype(BF16), cre_ref[m])
         - _dot(sim_ref[:, m * w_state:(m + 1) * w_state].astype(BF16), cim_ref[m]) for m in range(n_slab)], axis=1)
    y = y + d_ref[...] * u
    z = _dot(y.astype(BF16), wglu_ref[...])
    o_ref[...] = z[:, :d_model] * jax.nn.sigmoid(z[:, d_model:])


def _s5(u, h0r, h0i, prm, bg, tm):
    rows, c_c = u.shape
    n_state = h0r.shape[1]
    d_model = prm["w_glu"].shape[1] // 2
    consts = [prm["lam_re"], prm["lam_im"], prm["log_dt"], prm["b_re"], prm["b_im"], prm["c_re"], prm["c_im"],
              prm["d"], prm["w_glu"]]
    return pl.pallas_call(
        functools.partial(_s5_kernel, bg=bg),
        grid=(rows // tm,),
        in_specs=[pl.BlockSpec((tm, c_c), lambda i: (i, 0)), _const_spec(h0r.shape), _const_spec(h0i.shape)]
                 + [_const_spec(c.shape) for c in consts],
        out_specs=[pl.BlockSpec((tm, d_model), lambda i: (i, 0)), _const_spec(h0r.shape), _const_spec(h0i.shape)],
        out_shape=[jax.ShapeDtypeStruct((rows, d_model), F32), jax.ShapeDtypeStruct(h0r.shape, F32),
                   jax.ShapeDtypeStruct(h0i.shape, F32)],
        scratch_shapes=[pltpu.VMEM((tm, n_state), F32), pltpu.VMEM((tm, n_state), F32),
                        pltpu.VMEM((SUBLANES, n_state), F32), pltpu.VMEM(h0r.shape, F32),
                        pltpu.VMEM(h0i.shape, F32)],
        compiler_params=_params("arbitrary"),
        name="s5",
    )(u, h0r, h0i, *consts)


def _merge_kernel(zy_ref, bonus_ref, g_ref, yb_ref, yc_ref, pg_ref, x_ref, lng_ref, lnb_ref, ones_ref, woa_ref,
                  wob_ref, bgate_ref, wout_ref, o_ref, *, bg):
    d_model = x_ref.shape[1]
    ones_bd = ones_ref[...]
    y = _from_chain_minor(zy_ref, bg, bonus_ref.shape[1] // HEAD)
    cen = y - _head_sum(y, ones_bd) * (1.0 / HEAD)
    var = _head_sum(cen * cen, ones_bd) * (1.0 / HEAD)
    yn = cen * lax.rsqrt(var + GN_EPS) * lng_ref[...] + lnb_ref[...]
    y_a = _dot(((yn + bonus_ref[...]) * g_ref[...]).astype(BF16), woa_ref[...])
    y_b = _dot(yb_ref[...], wob_ref[...])
    gates = jax.nn.sigmoid(pg_ref[...] + bgate_ref[...])
    h = (gates[:, :d_model] * y_a + gates[:, d_model:2 * d_model] * y_b + gates[:, 2 * d_model:] * yc_ref[...])
    o_ref[...] = x_ref[...] + _dot(h.astype(BF16), wout_ref[...])


def _merge(zy, bonus, g, yb, yc, pg, x, prm, bg, tm):
    rows, d_model = x.shape
    acts = [bonus, g, yb, yc, pg, x]
    consts = [prm["ln_g"], prm["ln_b"], prm["ones_bd"], prm["rwkv_w_out"], prm["conf_w_out"], prm["b_gate"],
              prm["w_out"]]
    return pl.pallas_call(
        functools.partial(_merge_kernel, bg=bg),
        grid=(rows // tm,),
        in_specs=[pl.BlockSpec((tm // bg,) + zy.shape[1:], lambda i: (i, 0, 0))]
                 + [pl.BlockSpec((tm, a.shape[1]), lambda i: (i, 0)) for a in acts]
                 + [_const_spec(c.shape) for c in consts],
        out_specs=pl.BlockSpec((tm, d_model), lambda i: (i, 0)),
        out_shape=jax.ShapeDtypeStruct((rows, d_model), F32),
        compiler_params=_params("arbitrary"),
        name="merge",
    )(zy, *acts, *consts)


def _ffn_kernel(x_ref, st_ref, g_ref, wu_ref, dw_ref, db_ref, wd_ref, gf_ref, o_ref, carry_ref, *full_refs, bg, n_tap,
                d_ff, final_norm):
    tm = x_ref.shape[0]
    halo = (n_tap - 1) * bg

    @pl.when(pl.program_id(0) == 0)
    def _():
        carry_ref[...] = st_ref[...]

    x = x_ref[...]
    xn = _rms_scale(x, g_ref[...]).astype(BF16)
    acts = []
    for ci, lo in enumerate(range(0, d_ff, MXU_DIM)):
        wd = min(MXU_DIM, d_ff - lo)
        halves = []
        for h in range(2):
            cols = slice(h * d_ff + lo, h * d_ff + lo + wd)
            full_ref = full_refs[(2 * ci + h) % len(full_refs)]
            full_ref[0:halo, 0:wd] = carry_ref[:, cols]
            full_ref[halo:halo + tm, 0:wd] = _dot(xn, wu_ref[:, cols])
            c = jnp.zeros((tm, wd), F32) + db_ref[:, cols]
            for tap in range(n_tap):
                c = c + dw_ref[pl.ds(tap, 1), cols] * full_ref[tap * bg:tap * bg + tm, 0:wd]
            carry_ref[:, cols] = full_ref[tm:tm + halo, 0:wd]
            halves.append(c)
        acts.append((halves[0] * jax.nn.sigmoid(halves[0]) * halves[1]).astype(BF16))
    out = x + _dot(jnp.concatenate(acts, axis=1), wd_ref[...])
    o_ref[...] = _rms_scale(out, gf_ref[...]) if final_norm else out


FFN_STAGING_BUFFERS = 4


def _conv_ffn(x, st0, prm, final_g, bg, tm, final_norm):
    rows, d_model = x.shape
    d_ff = prm["w_down"].shape[0]
    n_tap = prm["dw_w"].shape[0]
    halo = (n_tap - 1) * bg
    resident = lambda a: pl.BlockSpec(a.shape, lambda i: (0,) * a.ndim, pipeline_mode=pl.Buffered(1))
    return pl.pallas_call(
        functools.partial(_ffn_kernel, bg=bg, n_tap=n_tap, d_ff=d_ff, final_norm=final_norm),
        grid=(rows // tm,),
        in_specs=[pl.BlockSpec((tm, d_model), lambda i: (i, 0)), resident(st0), _const_spec(prm["norm_g"].shape),
                  resident(prm["w_up"]), _const_spec(prm["dw_w"].shape), _const_spec(prm["dw_b"].shape),
                  resident(prm["w_down"]), _const_spec(final_g.shape)],
        out_specs=[pl.BlockSpec((tm, d_model), lambda i: (i, 0)), _const_spec(st0.shape)],
        out_shape=[jax.ShapeDtypeStruct((rows, d_model), F32), jax.ShapeDtypeStruct(st0.shape, F32)],
        scratch_shapes=[pltpu.VMEM((halo + tm, MXU_DIM), F32)] * FFN_STAGING_BUFFERS,
        compiler_params=_params("arbitrary"),
        name="conv_ffn",
    )(x, st0, prm["norm_g"], prm["w_up"], prm["dw_w"], prm["dw_b"], prm["w_down"], final_g)


def _tile(rows, bg, pref):
    tm = max(bg, min(pref, rows))
    while rows % tm or tm % bg:
        tm -= bg
    return tm


def _time_major(st):
    return st.transpose(1, 0, 2).reshape(-1, st.shape[-1])


def _run_trunk(x_bm, states, layers, final_g):
    bg, n_t, d_model = x_bm.shape
    rows = bg * n_t
    x = x_bm.transpose(1, 0, 2).reshape(rows, d_model)
    shift, wkv, conv, ssm_re, ssm_im, ffn_conv = states
    outs = ([], [], [], [], [], [])
    for l, prm in enumerate(layers):
        p_c, p_s, p_g, g, bonus, wkv_operands, shift_n = _in_proj(
            x, prm["norm_mix_g"], prm["w_in"], prm["in_widths"], shift[l], prm["rwkv"], bg,
            _tile(rows, 2 * bg, 256))
        zy, wkv_n = _wkv_group(wkv_operands, wkv[l], bg, n_t)
        yb, conv_n = _conformer(p_c, _time_major(conv[l]), prm["conf"], bg, _tile(rows, bg, 1024 if bg > 8 else 512))
        n_state = ssm_re.shape[2] * ssm_re.shape[3]
        yc, re_n, im_n = _s5(p_s, ssm_re[l].reshape(bg, n_state), ssm_im[l].reshape(bg, n_state), prm["s5"], bg,
                             _tile(rows, bg, 256 if bg > 8 else 512))
        x = _merge(zy, bonus, g, yb, yc, p_g, x, prm["merge"], bg, _tile(rows, 2 * bg, 512))
        x, ffn_n = _conv_ffn(x, _time_major(ffn_conv[l]), prm["ffn"], final_g, bg, _tile(rows, bg, 512),
                             final_norm=(l == len(layers) - 1))
        for lst, s in zip(outs, (shift_n, wkv_n, conv_n, re_n, im_n, ffn_n)):
            lst.append(s)
    y = x.reshape(n_t, bg, d_model).transpose(1, 0, 2)
    shift_n, wkv_n, conv_n, re_n, im_n, ffn_n = (jnp.stack(lst) for lst in outs)
    depth = len(layers)
    new_states = [shift_n, _wkv_states_from_chains(wkv_n, bg, wkv.shape[2]),
                  conv_n.reshape(depth, -1, bg, conv_n.shape[-1]).transpose(0, 2, 1, 3),
                  re_n.reshape(ssm_re.shape), im_n.reshape(ssm_im.shape),
                  ffn_n.reshape(depth, -1, bg, ffn_n.shape[-1]).transpose(0, 2, 1, 3)]
    return y, new_states


def _block_diag(blocks):
    n_g, a, b = blocks.shape
    eye = jnp.eye(n_g, dtype=blocks.dtype)
    return (eye[:, None, :, None] * blocks[:, :, None, :]).reshape(n_g * a, n_g * b)


def _prep_layer(l, w):
    row = lambda v: v.reshape(1, -1).astype(F32)
    c_a = w["rwkv_w0"].shape[1]
    n_rwkv = w["rwkv_mu"].shape[1]
    c_b = w["conf_dw_b"].shape[1]
    c_c = w["s5_d"].shape[1]
    d_model = w["w_out"].shape[1]
    lora_w = w["rwkv_w_up"].shape[1]
    lora_a = w["rwkv_a_up"].shape[1]
    assert lora_w + lora_a == LORA_PAIR and n_rwkv == 3 * c_a + LORA_PAIR + w["rwkv_g_up"].shape[1]
    n_grp, p_c = w["s5_lambda_re"].shape[1:]
    grp_per_slab = LANES // (c_c // n_grp)

    def slabs(blocks):
        t = blocks.transpose(0, 2, 1)
        t = t.reshape(n_grp // grp_per_slab, grp_per_slab, *t.shape[1:])
        return jax.vmap(_block_diag)(t).astype(BF16)

    assert c_a % MXU_DIM == 0
    ones_bd = _block_diag(jnp.ones((MXU_DIM // HEAD, HEAD, HEAD), F32)).astype(BF16)
    rwkv = dict(
        mu=row(w["rwkv_mu"][l]), w0=row(w["rwkv_w0"][l]), a0=row(w["rwkv_a0"][l]), k_k=row(w["rwkv_k_k"][l]),
        k_a=row(w["rwkv_k_a"][l]), r_k=row(w["rwkv_r_k"][l]),
        w_up=jnp.concatenate([w["rwkv_w_up"][l], jnp.zeros((lora_a, c_a), F32)], axis=0).astype(BF16),
        a_up=jnp.concatenate([jnp.zeros((lora_w, c_a), F32), w["rwkv_a_up"][l]], axis=0).astype(BF16),
        g_up=w["rwkv_g_up"][l].astype(BF16), ones_bd=ones_bd)
    conf = dict(dw_w=w["conf_dw_w"][l], dw_b=row(w["conf_dw_b"][l]), ln_g=row(w["conf_ln_g"][l]),
                ln_b=row(w["conf_ln_b"][l]))
    s5 = dict(
        lam_re=row(w["s5_lambda_re"][l]), lam_im=row(w["s5_lambda_im"][l]),
        log_dt=row(jnp.repeat(w["s5_log_dt"][l], p_c)),
        b_re=slabs(w["s5_b_re"][l]), b_im=slabs(w["s5_b_im"][l]),
        c_re=slabs(w["s5_c_re"][l]), c_im=slabs(w["s5_c_im"][l]),
        d=row(w["s5_d"][l]), w_glu=w["s5_w_glu"][l].astype(BF16))
    merge = dict(ln_g=row(w["rwkv_ln_g"][l]), ln_b=row(w["rwkv_ln_b"][l]), ones_bd=ones_bd,
                 rwkv_w_out=w["rwkv_w_out"][l].astype(BF16), conf_w_out=w["conf_w_out"][l].astype(BF16),
                 b_gate=row(w["b_gate"][l]), w_out=w["w_out"][l].astype(BF16))
    ffn = dict(norm_g=row(w["norm_ffn_g"][l]), w_up=w["ffn_w_up"][l].astype(BF16), dw_w=w["ffn_dw_w"][l],
               dw_b=row(w["ffn_dw_b"][l]), w_down=w["ffn_w_down"][l].astype(BF16))
    in_widths = (n_rwkv, 2 * c_b, c_c, w["b_gate"].shape[1])
    assert sum(in_widths) == w["w_in"].shape[2] and d_model * 3 == in_widths[3] and n_grp * p_c > 0
    return dict(norm_mix_g=row(w["norm_mix_g"][l]), w_in=w["w_in"][l].astype(BF16), in_widths=in_widths,
                rwkv=rwkv, conf=conf, s5=s5, merge=merge, ffn=ffn)


def kernel(x_prompt, x_sample, state_shift, state_wkv, state_conv, state_ssm_re, state_ssm_im, state_ffn_conv, norm_mix_g, w_in, b_gate, rwkv_mu, rwkv_w0, rwkv_w_up, rwkv_a0, rwkv_a_up, rwkv_g_up, rwkv_k_k, rwkv_k_a, rwkv_r_k, rwkv_ln_g, rwkv_ln_b, rwkv_w_out, conf_dw_w, conf_dw_b, conf_ln_g, conf_ln_b, conf_w_out, s5_lambda_re, s5_lambda_im, s5_log_dt, s5_b_re, s5_b_im, s5_c_re, s5_c_im, s5_d, s5_w_glu, w_out, norm_ffn_g, ffn_w_up, ffn_dw_w, ffn_dw_b, ffn_w_down, norm_final_g):
    w = dict(norm_mix_g=norm_mix_g, w_in=w_in, b_gate=b_gate, rwkv_mu=rwkv_mu, rwkv_w0=rwkv_w0,
             rwkv_w_up=rwkv_w_up, rwkv_a0=rwkv_a0, rwkv_a_up=rwkv_a_up, rwkv_g_up=rwkv_g_up, rwkv_k_k=rwkv_k_k,
             rwkv_k_a=rwkv_k_a, rwkv_r_k=rwkv_r_k, rwkv_ln_g=rwkv_ln_g, rwkv_ln_b=rwkv_ln_b,
             rwkv_w_out=rwkv_w_out, conf_dw_w=conf_dw_w, conf_dw_b=conf_dw_b, conf_ln_g=conf_ln_g,
             conf_ln_b=conf_ln_b, conf_w_out=conf_w_out, s5_lambda_re=s5_lambda_re, s5_lambda_im=s5_lambda_im,
             s5_log_dt=s5_log_dt, s5_b_re=s5_b_re, s5_b_im=s5_b_im, s5_c_re=s5_c_re, s5_c_im=s5_c_im, s5_d=s5_d,
             s5_w_glu=s5_w_glu, w_out=w_out, norm_ffn_g=norm_ffn_g, ffn_w_up=ffn_w_up, ffn_dw_w=ffn_dw_w,
             ffn_dw_b=ffn_dw_b, ffn_w_down=ffn_w_down)
    depth = w_in.shape[0]
    layers = [_prep_layer(l, w) for l in range(depth)]
    final_g = norm_final_g.reshape(1, -1)
    bp = x_prompt.shape[0]
    zeros_like_state = lambda s: jnp.zeros((depth, bp) + s.shape[2:], s.dtype)
    prompt_states = tuple(zeros_like_state(s) for s in
                          (state_shift, state_wkv, state_conv, state_ssm_re, state_ssm_im, state_ffn_conv))
    y_prompt, p_states = _run_trunk(x_prompt, prompt_states, layers, final_g)
    y_sample, s_states = _run_trunk(
        x_sample, (state_shift, state_wkv, state_conv, state_ssm_re, state_ssm_im, state_ffn_conv), layers, final_g)
    return (y_prompt, y_sample, *p_states, *s_states)
```

```python
import functools
import math

import jax
import jax.numpy as jnp
from jax import lax
from jax.experimental import pallas as pl
from jax.experimental.pallas import tpu as pltpu

F32 = jnp.float32
BF16 = jnp.bfloat16

VMEM_LIMIT_BYTES = 56 * 1024 * 1024
SUBLANES = 8
LANES = 128
MXU_DIM = 256

HEAD = 64
LORA_PAIR = 128
RMS_EPS = 1e-6
LN_EPS = 1e-5
GN_EPS = 64e-5
DECAY_SCALE = math.exp(-0.5)


def _params(*sem):
    return pltpu.CompilerParams(dimension_semantics=sem, vmem_limit_bytes=VMEM_LIMIT_BYTES)


def _const_spec(shape):
    zeros = (0,) * len(shape)
    return pl.BlockSpec(shape, lambda *_: zeros)


def _dot(a, b):
    return jnp.dot(a, b, preferred_element_type=F32)


def _head_sum(x, ones_bd):
    hi = x.astype(BF16)
    lo = (x - hi.astype(F32)).astype(BF16)
    wd = ones_bd.shape[0]
    return jnp.concatenate(
        [_dot(hi[:, s:s + wd], ones_bd) + _dot(lo[:, s:s + wd], ones_bd) for s in range(0, x.shape[1], wd)], axis=1)


def _rms_scale(x, g):
    ms = jnp.mean(x * x, axis=-1, keepdims=True)
    return x * lax.rsqrt(ms + RMS_EPS) * g


def _chain_geometry(bg, n_h):
    dup = max(1, LANES // (bg * n_h))
    nb = LANES // (n_h * dup)
    assert n_h % 2 == 0 and nb % SUBLANES == 0 and bg % nb == 0
    return dup, nb, HEAD // dup, bg // nb


def _swap_halves(x):
    return pltpu.roll(x, HEAD, 1)


def _to_chain_minor(x, dst_ref, bg, n_h):
    dup, nb, _, n_blk = _chain_geometry(bg, n_h)
    low = lax.broadcasted_iota(jnp.int32, (nb, LANES), 1) < HEAD
    for t0 in range(0, x.shape[0] // bg, 2):
        for c in range(n_blk):
            r0 = t0 * bg + c * nb
            r1 = r0 + bg
            pieces = []
            for q in range(n_h // 2):
                x0 = x[r0:r0 + nb, q * LANES:(q + 1) * LANES]
                x1 = x[r1:r1 + nb, q * LANES:(q + 1) * LANES]
                pieces.append(jnp.where(low, x0, _swap_halves(x1)))
                pieces.append(jnp.where(low, _swap_halves(x0), x1))
            tt = jnp.concatenate(pieces * dup, axis=0).T
            dst_ref[t0, :, c * LANES:(c + 1) * LANES] = tt[:HEAD]
            dst_ref[t0 + 1, :, c * LANES:(c + 1) * LANES] = tt[HEAD:]


def _from_chain_minor(src_ref, bg, n_h):
    dup, nb, _, n_blk = _chain_geometry(bg, n_h)
    n_step = src_ref.shape[0]
    low = lax.broadcasted_iota(jnp.int32, (nb, LANES), 1) < HEAD
    tiles = {}
    for t0 in range(0, n_step, 2):
        for c in range(n_blk):
            ys = []
            for t in (t0, t0 + 1):
                y = src_ref[t, :, c * LANES:(c + 1) * LANES]
                ys += [y] if dup == 1 else [y, _swap_halves(y)]
            wt = jnp.concatenate(ys, axis=0).T
            first, second = [], []
            for q in range(n_h // 2):
                even = wt[2 * q * nb:(2 * q + 1) * nb]
                odd = wt[(2 * q + 1) * nb:(2 * q + 2) * nb]
                first.append(jnp.where(low, even, _swap_halves(odd)))
                second.append(jnp.where(low, _swap_halves(even), odd))
            tiles[t0, c] = jnp.concatenate(first, axis=1)
            tiles[t0 + 1, c] = jnp.concatenate(second, axis=1)
    return jnp.concatenate([tiles[t, c] for t in range(n_step) for c in range(n_blk)], axis=0)


def _in_proj_kernel(x_ref, g_ref, w_ref, s0_ref, mu_ref, w0_ref, a0_ref, kk_ref, ka_ref, rk_ref, wup_ref, aup_ref,
                    gup_ref, ones_ref, pc_o, ps_o, pg_o, g_o, bonus_o, r_o, w_o, k_o, v_o, a_o, b_o, shift_o,
                    carry_ref, *, bg, c_a, bounds):
    @pl.when(pl.program_id(0) == 0)
    def _():
        carry_ref[...] = s0_ref[...]

    xn = _rms_scale(x_ref[...], g_ref[...]).astype(BF16)
    p = _dot(xn, w_ref[:, bounds[0][0]:bounds[0][1]])
    tm = p.shape[0]
    prev = carry_ref[...] if tm == bg else jnp.concatenate([carry_ref[...], p[:tm - bg]], axis=0)
    carry_ref[...] = p[tm - bg:]
    shift_o[...] = p[tm - bg:]
    xm = p + (prev - p) * mu_ref[...]
    r = xm[:, 0:c_a]
    k = xm[:, c_a:2 * c_a]
    v = xm[:, 2 * c_a:3 * c_a]
    xwa = xm[:, 3 * c_a:3 * c_a + LORA_PAIR]
    xg = xm[:, 3 * c_a + LORA_PAIR:]
    ones_bd = ones_ref[...]
    z = w0_ref[...] + _dot(jnp.tanh(xwa).astype(BF16), wup_ref[...])
    decay = jnp.exp(-DECAY_SCALE * jax.nn.sigmoid(z))
    a = jax.nn.sigmoid(a0_ref[...] + _dot(xwa.astype(BF16), aup_ref[...]))
    g = _dot(jax.nn.sigmoid(xg).astype(BF16), gup_ref[...])
    kk = k * kk_ref[...]
    kk = kk * lax.rsqrt(jnp.maximum(_head_sum(kk * kk, ones_bd), 1e-24))
    k_h = k * (1.0 + (a - 1.0) * ka_ref[...])
    g_o[...] = g
    bonus_o[...] = _head_sum(r * k_h * rk_ref[...], ones_bd) * v
    n_h = c_a // HEAD
    for val, dst_ref in ((r, r_o), (decay, w_o), (k_h, k_o), (v, v_o), (-kk, a_o), (kk * a, b_o)):
        _to_chain_minor(val, dst_ref, bg, n_h)
    for o_ref, (lo, hi) in zip((pc_o, ps_o, pg_o), bounds[1:]):
        o_ref[...] = _dot(xn, w_ref[:, lo:hi])


def _in_proj(x, g, w_bf16, widths, shift0, prm, bg, tm):
    rows, d = x.shape
    c_a = prm["w0"].shape[1]
    bounds, lo = [], 0
    for wd in widths:
        bounds.append((lo, lo + wd))
        lo += wd
    consts = [prm["mu"], prm["w0"], prm["a0"], prm["k_k"], prm["k_a"], prm["r_k"], prm["w_up"], prm["a_up"],
              prm["g_up"], prm["ones_bd"]]
    out_w = list(widths[1:]) + [c_a] * 2
    n_chain = _chain_geometry(bg, c_a // HEAD)[3] * LANES
    outs = pl.pallas_call(
        functools.partial(_in_proj_kernel, bg=bg, c_a=c_a, bounds=tuple(bounds)),
        grid=(rows // tm,),
        in_specs=[pl.BlockSpec((tm, d), lambda i: (i, 0)), _const_spec(g.shape),
                  pl.BlockSpec(w_bf16.shape, lambda i: (0, 0), pipeline_mode=pl.Buffered(1)),
                  _const_spec(shift0.shape)] + [_const_spec(c.shape) for c in consts],
        out_specs=[pl.BlockSpec((tm, wd), lambda i: (i, 0)) for wd in out_w]
                  + [pl.BlockSpec((tm // bg, HEAD, n_chain), lambda i: (i, 0, 0))] * 6 + [_const_spec(shift0.shape)],
        out_shape=[jax.ShapeDtypeStruct((rows, wd), F32) for wd in out_w]
                  + [jax.ShapeDtypeStruct((rows // bg, HEAD, n_chain), F32)] * 6
                  + [jax.ShapeDtypeStruct(shift0.shape, F32)],
        scratch_shapes=[pltpu.VMEM(shift0.shape, F32)],
        compiler_params=_params("arbitrary"),
        name="in_proj",
    )(x, g, w_bf16, shift0, *consts)
    return outs[0], outs[1], outs[2], outs[3], outs[4], outs[5:11], outs[11]


def _wkv_kernel(zr, zw, zk, zv, za, zb, s0_ref, zy_ref, st_ref, *, n_i, tc):
    dup = HEAD // n_i
    low_i = lax.broadcasted_iota(jnp.int32, (n_i, LANES), 1) < HEAD

    def row(ref, t, j):
        return ref[t, pl.ds(j, 1), :]

    @pl.when(pl.program_id(1) == 0)
    def _():
        st_ref[...] = s0_ref[...]

    sa0 = jnp.zeros((n_i, LANES), F32)
    for j in range(HEAD):
        sa0 = sa0 + st_ref[j] * row(za, 0, j)

    def step(t, sa):
        t_next = jnp.minimum(t + 1, tc - 1)
        v = zv[t] if dup == 1 else jnp.where(low_i, zv[t, :n_i, :], zv[t, n_i:, :])
        y = jnp.zeros((n_i, LANES), F32)
        sa_next = jnp.zeros((n_i, LANES), F32)
        for j in range(HEAD):
            s = st_ref[j] * row(zw, t, j) + sa * row(zb, t, j) + v * row(zk, t, j)
            st_ref[j] = s
            y = y + s * row(zr, t, j)
            sa_next = sa_next + s * row(za, t_next, j)
        zy_ref[t] = y
        return sa_next

    lax.fori_loop(0, tc, step, sa0)


def _wkv_group(operands, wkv0, bg, n_t):
    n_h = wkv0.shape[1]
    dup, nb, n_i, n_blk = _chain_geometry(bg, n_h)
    tc = min(n_t, 64)
    assert n_t % tc == 0
    s0 = wkv0.reshape(n_blk, nb, n_h, dup, n_i, HEAD).transpose(5, 4, 0, 3, 2, 1).reshape(HEAD, n_i, n_blk * LANES)
    seq = pl.BlockSpec((tc, HEAD, LANES), lambda c, t: (t, 0, c))
    state = pl.BlockSpec((HEAD, n_i, LANES), lambda c, t: (0, 0, c))
    return pl.pallas_call(
        functools.partial(_wkv_kernel, n_i=n_i, tc=tc),
        grid=(n_blk, n_t // tc),
        in_specs=[seq] * 6 + [state],
        out_specs=[pl.BlockSpec((tc, n_i, LANES), lambda c, t: (t, 0, c)), state],
        out_shape=[jax.ShapeDtypeStruct((n_t, n_i, n_blk * LANES), F32), jax.ShapeDtypeStruct(s0.shape, F32)],
        compiler_params=_params("arbitrary", "arbitrary"),
        name="wkv",
    )(*operands, s0)


def _wkv_states_from_chains(st, bg, n_h):
    depth, _, n_i, n_c = st.shape
    dup = HEAD // n_i
    nb = LANES // (n_h * dup)
    st = st.reshape(depth, HEAD, n_i, n_c // LANES, dup, n_h, nb).transpose(0, 3, 6, 5, 4, 2, 1)
    return st.reshape(depth, bg, n_h, HEAD, HEAD)


def _conf_kernel(p_ref, st0_ref, dww_ref, dwb_ref, lng_ref, lnb_ref, o_ref, st_o_ref, full_ref, *, bg, n_tap,
                 carry):
    tm, c2 = p_ref.shape
    c_b = c2 // 2
    halo = (n_tap - 1) * bg

    @pl.when(pl.program_id(0) == 0)
    def _():
        full_ref[0:halo, :] = st0_ref[...]

    p = p_ref[...]
    full_ref[halo:halo + tm, :] = p[:, :c_b] * jax.nn.sigmoid(p[:, c_b:])
    acc = jnp.zeros((tm, c_b), F32) + dwb_ref[...]
    for tap in range(n_tap):
        acc = acc + dww_ref[pl.ds(tap, 1), :] * full_ref[tap * bg:tap * bg + tm, :]
    mean = jnp.mean(acc, axis=-1, keepdims=True)
    cen = acc - mean
    var = jnp.mean(cen * cen, axis=-1, keepdims=True)
    n = cen * lax.rsqrt(var + LN_EPS) * lng_ref[...] + lnb_ref[...]
    o_ref[...] = (n * jax.nn.sigmoid(n)).astype(o_ref.dtype)
    tail = full_ref[tm:tm + halo, :]
    st_o_ref[...] = tail
    if carry:
        full_ref[0:halo, :] = tail


def _conformer(p_c, st0, prm, bg, tm):
    rows, c2 = p_c.shape
    c_b = c2 // 2
    n_tap = prm["dw_w"].shape[0]
    halo = (n_tap - 1) * bg
    consts = [prm["dw_w"], prm["dw_b"], prm["ln_g"], prm["ln_b"]]
    return pl.pallas_call(
        functools.partial(_conf_kernel, bg=bg, n_tap=n_tap, carry=rows > tm),
        grid=(rows // tm,),
        in_specs=[pl.BlockSpec((tm, c2), lambda i: (i, 0)), _const_spec(st0.shape)]
                 + [_const_spec(c.shape) for c in consts],
        out_specs=[pl.BlockSpec((tm, c_b), lambda i: (i, 0)), _const_spec((halo, c_b))],
        out_shape=[jax.ShapeDtypeStruct((rows, c_b), BF16), jax.ShapeDtypeStruct((halo, c_b), F32)],
        scratch_shapes=[pltpu.VMEM((halo + tm, c_b), F32)],
        compiler_params=_params("arbitrary"),
        name="conformer",
    )(p_c, st0, *consts)


def _s5_kernel(u_ref, h0r_ref, h0i_ref, lamr_ref, lami_ref, dt_ref, bre_ref, bim_ref, cre_ref, cim_ref, d_ref,
               wglu_ref, o_ref, hr_o, hi_o, sre_ref, sim_ref, coef_ref, hr_s, hi_s, *, bg):
    tm = u_ref.shape[0]
    n_state = sre_ref.shape[1]
    d_model = o_ref.shape[1]

    @pl.when(pl.program_id(0) == 0)
    def _():
        lr = lamr_ref[...]
        li = lami_ref[...]
        dt = jnp.exp(dt_ref[...])
        mag = jnp.exp(lr * dt)
        ab_re = mag * jnp.cos(li * dt)
        ab_im = mag * jnp.sin(li * dt)
        den = lr * lr + li * li
        coef_ref[0:1, :] = ab_re
        coef_ref[1:2, :] = ab_im
        coef_ref[2:3, :] = ((ab_re - 1.0) * lr + ab_im * li) / den
        coef_ref[3:4, :] = (ab_im * lr - (ab_re - 1.0) * li) / den
        hr_s[...] = h0r_ref[...]
        hi_s[...] = h0i_ref[...]

    u = u_ref[...]
    ub = u.astype(BF16)
    n_slab = bre_ref.shape[0]
    w_state = n_state // n_slab
    n_step = tm // bg
    ys = []
    for m in range(n_slab):
        um = ub[:, m * LANES:(m + 1) * LANES]
        st = slice(m * w_state, (m + 1) * w_state)
        bu_re = _dot(um, bre_ref[m])
        bu_im = _dot(um, bim_ref[m])
        q_re = coef_ref[2:3, st]
        q_im = coef_ref[3:4, st]
        sre_ref[:, st] = q_re * bu_re - q_im * bu_im
        sim_ref[:, st] = q_re * bu_im + q_im * bu_re
        ab_re = jnp.broadcast_to(coef_ref[0:1, st], (SUBLANES, w_state))
        ab_im = jnp.broadcast_to(coef_ref[1:2, st], (SUBLANES, w_state))
        for r0 in range(0, bg, SUBLANES):
            s_re = hr_s[r0:r0 + SUBLANES, st]
            s_im = hi_s[r0:r0 + SUBLANES, st]
            for t in range(n_step):
                rows = slice(t * bg + r0, t * bg + r0 + SUBLANES)
                s_re, s_im = (ab_re * s_re - ab_im * s_im + sre_ref[rows, st],
                              ab_re * s_im + ab_im * s_re + sim_ref[rows, st])
                sre_ref[rows, st] = s_re
                sim_ref[rows, st] = s_im
            hr_s[r0:r0 + SUBLANES, st] = s_re
            hi_s[r0:r0 + SUBLANES, st] = s_im
        ys.append(_dot(sre_ref[:, st].astype(BF16), cre_ref[m]) - _dot(sim_ref[:, st].astype(BF16), cim_ref[m]))

    hr_o[...] = hr_s[...]
    hi_o[...] = hi_s[...]
    y = jnp.concatenate(ys, axis=1) + d_ref[...] * u
    z = _dot(y.astype(BF16), wglu_ref[...])
    o_ref[...] = z[:, :d_model] * jax.nn.sigmoid(z[:, d_model:])


def _s5(u, h0r, h0i, prm, bg, tm):
    rows, c_c = u.shape
    n_state = h0r.shape[1]
    d_model = prm["w_glu"].shape[1] // 2
    consts = [prm["lam_re"], prm["lam_im"], prm["log_dt"], prm["b_re"], prm["b_im"], prm["c_re"], prm["c_im"],
              prm["d"], prm["w_glu"]]
    return pl.pallas_call(
        functools.partial(_s5_kernel, bg=bg),
        grid=(rows // tm,),
        in_specs=[pl.BlockSpec((tm, c_c), lambda i: (i, 0)), _const_spec(h0r.shape), _const_spec(h0i.shape)]
                 + [_const_spec(c.shape) for c in consts],
        out_specs=[pl.BlockSpec((tm, d_model), lambda i: (i, 0)), _const_spec(h0r.shape), _const_spec(h0i.shape)],
        out_shape=[jax.ShapeDtypeStruct((rows, d_model), F32), jax.ShapeDtypeStruct(h0r.shape, F32),
                   jax.ShapeDtypeStruct(h0i.shape, F32)],
        scratch_shapes=[pltpu.VMEM((tm, n_state), F32), pltpu.VMEM((tm, n_state), F32),
                        pltpu.VMEM((SUBLANES, n_state), F32), pltpu.VMEM(h0r.shape, F32),
                        pltpu.VMEM(h0i.shape, F32)],
        compiler_params=_params("arbitrary"),
        name="s5",
    )(u, h0r, h0i, *consts)


def _merge_kernel(zy_ref, bonus_ref, g_ref, yb_ref, yc_ref, pg_ref, x_ref, lng_ref, lnb_ref, ones_ref, woa_ref,
                  wob_ref, bgate_ref, wout_ref, o_ref, *, bg):
    d_model = x_ref.shape[1]
    ones_bd = ones_ref[...]
    y = _from_chain_minor(zy_ref, bg, bonus_ref.shape[1] // HEAD)
    cen = y - _head_sum(y, ones_bd) * (1.0 / HEAD)
    var = _head_sum(cen * cen, ones_bd) * (1.0 / HEAD)
    yn = cen * lax.rsqrt(var + GN_EPS) * lng_ref[...] + lnb_ref[...]
    y_a = _dot(((yn + bonus_ref[...]) * g_ref[...]).astype(BF16), woa_ref[...])
    y_b = _dot(yb_ref[...], wob_ref[...])
    gates = jax.nn.sigmoid(pg_ref[...] + bgate_ref[...])
    h = (gates[:, :d_model] * y_a + gates[:, d_model:2 * d_model] * y_b + gates[:, 2 * d_model:] * yc_ref[...])
    o_ref[...] = x_ref[...] + _dot(h.astype(BF16), wout_ref[...])


def _merge(zy, bonus, g, yb, yc, pg, x, prm, bg, tm):
    rows, d_model = x.shape
    acts = [bonus, g, yb, yc, pg, x]
    consts = [prm["ln_g"], prm["ln_b"], prm["ones_bd"], prm["rwkv_w_out"], prm["conf_w_out"], prm["b_gate"],
              prm["w_out"]]
    return pl.pallas_call(
        functools.partial(_merge_kernel, bg=bg),
        grid=(rows // tm,),
        in_specs=[pl.BlockSpec((tm // bg,) + zy.shape[1:], lambda i: (i, 0, 0))]
                 + [pl.BlockSpec((tm, a.shape[1]), lambda i: (i, 0)) for a in acts]
                 + [_const_spec(c.shape) for c in consts],
        out_specs=pl.BlockSpec((tm, d_model), lambda i: (i, 0)),
        out_shape=jax.ShapeDtypeStruct((rows, d_model), F32),
        compiler_params=_params("arbitrary"),
        name="merge",
    )(zy, *acts, *consts)


def _ffn_kernel(x_ref, st_ref, g_ref, wu_ref, dw_ref, db_ref, wd_ref, gf_ref, o_ref, carry_ref, *full_refs, bg, n_tap,
                d_ff, final_norm):
    tm = x_ref.shape[0]
    halo = (n_tap - 1) * bg

    @pl.when(pl.program_id(0) == 0)
    def _():
        carry_ref[...] = st_ref[...]

    x = x_ref[...]
    xn = _rms_scale(x, g_ref[...]).astype(BF16)
    acts = []
    for ci, lo in enumerate(range(0, d_ff, MXU_DIM)):
        wd = min(MXU_DIM, d_ff - lo)
        halves = []
        for h in range(2):
            cols = slice(h * d_ff + lo, h * d_ff + lo + wd)
            full_ref = full_refs[(2 * ci + h) % len(full_refs)]
            full_ref[0:halo, 0:wd] = carry_ref[:, cols]
            full_ref[halo:halo + tm, 0:wd] = _dot(xn, wu_ref[:, cols])
            c = jnp.zeros((tm, wd), F32) + db_ref[:, cols]
            for tap in range(n_tap):
                c = c + dw_ref[pl.ds(tap, 1), cols] * full_ref[tap * bg:tap * bg + tm, 0:wd]
            carry_ref[:, cols] = full_ref[tm:tm + halo, 0:wd]
            halves.append(c)
        acts.append((halves[0] * jax.nn.sigmoid(halves[0]) * halves[1]).astype(BF16))
    out = x + _dot(jnp.concatenate(acts, axis=1), wd_ref[...])
    o_ref[...] = _rms_scale(out, gf_ref[...]) if final_norm else out


FFN_STAGING_BUFFERS = 4


def _conv_ffn(x, st0, prm, final_g, bg, tm, final_norm):
    rows, d_model = x.shape
    d_ff = prm["w_down"].shape[0]
    n_tap = prm["dw_w"].shape[0]
    halo = (n_tap - 1) * bg
    resident = lambda a: pl.BlockSpec(a.shape, lambda i: (0,) * a.ndim, pipeline_mode=pl.Buffered(1))
    return pl.pallas_call(
        functools.partial(_ffn_kernel, bg=bg, n_tap=n_tap, d_ff=d_ff, final_norm=final_norm),
        grid=(rows // tm,),
        in_specs=[pl.BlockSpec((tm, d_model), lambda i: (i, 0)), resident(st0), _const_spec(prm["norm_g"].shape),
                  resident(prm["w_up"]), _const_spec(prm["dw_w"].shape), _const_spec(prm["dw_b"].shape),
                  resident(prm["w_down"]), _const_spec(final_g.shape)],
        out_specs=[pl.BlockSpec((tm, d_model), lambda i: (i, 0)), _const_spec(st0.shape)],
        out_shape=[jax.ShapeDtypeStruct((rows, d_model), F32), jax.ShapeDtypeStruct(st0.shape, F32)],
        scratch_shapes=[pltpu.VMEM((halo + tm, MXU_DIM), F32)] * FFN_STAGING_BUFFERS,
        compiler_params=_params("arbitrary"),
        name="conv_ffn",
    )(x, st0, prm["norm_g"], prm["w_up"], prm["dw_w"], prm["dw_b"], prm["w_down"], final_g)


def _tile(rows, bg, pref):
    tm = max(bg, min(pref, rows))
    while rows % tm or tm % bg:
        tm -= bg
    return tm


def _time_major(st):
    return st.transpose(1, 0, 2).reshape(-1, st.shape[-1])


def _run_trunk(x_bm, states, layers, final_g):
    bg, n_t, d_model = x_bm.shape
    rows = bg * n_t
    x = x_bm.transpose(1, 0, 2).reshape(rows, d_model)
    shift, wkv, conv, ssm_re, ssm_im, ffn_conv = states
    outs = ([], [], [], [], [], [])
    for l, prm in enumerate(layers):
        p_c, p_s, p_g, g, bonus, wkv_operands, shift_n = _in_proj(
            x, prm["norm_mix_g"], prm["w_in"], prm["in_widths"], shift[l], prm["rwkv"], bg,
            _tile(rows, 2 * bg, 256))
        zy, wkv_n = _wkv_group(wkv_operands, wkv[l], bg, n_t)
        yb, conv_n = _conformer(p_c, _time_major(conv[l]), prm["conf"], bg, _tile(rows, bg, 1024 if bg > 8 else 512))
        n_state = ssm_re.shape[2] * ssm_re.shape[3]
        yc, re_n, im_n = _s5(p_s, ssm_re[l].reshape(bg, n_state), ssm_im[l].reshape(bg, n_state), prm["s5"], bg,
                             _tile(rows, bg, 256 if bg > 8 else 512))
        x = _merge(zy, bonus, g, yb, yc, p_g, x, prm["merge"], bg, _tile(rows, 2 * bg, 512))
        x, ffn_n = _conv_ffn(x, _time_major(ffn_conv[l]), prm["ffn"], final_g, bg, _tile(rows, bg, 512),
                             final_norm=(l == len(layers) - 1))
        for lst, s in zip(outs, (shift_n, wkv_n, conv_n, re_n, im_n, ffn_n)):
            lst.append(s)
    y = x.reshape(n_t, bg, d_model).transpose(1, 0, 2)
    shift_n, wkv_n, conv_n, re_n, im_n, ffn_n = (jnp.stack(lst) for lst in outs)
    depth = len(layers)
    new_states = [shift_n, _wkv_states_from_chains(wkv_n, bg, wkv.shape[2]),
                  conv_n.reshape(depth, -1, bg, conv_n.shape[-1]).transpose(0, 2, 1, 3),
                  re_n.reshape(ssm_re.shape), im_n.reshape(ssm_im.shape),
                  ffn_n.reshape(depth, -1, bg, ffn_n.shape[-1]).transpose(0, 2, 1, 3)]
    return y, new_states


def _block_diag(blocks):
    n_g, a, b = blocks.shape
    eye = jnp.eye(n_g, dtype=blocks.dtype)
    return (eye[:, None, :, None] * blocks[:, :, None, :]).reshape(n_g * a, n_g * b)


def _prep_layer(l, w):
    row = lambda v: v.reshape(1, -1).astype(F32)
    c_a = w["rwkv_w0"].shape[1]
    n_rwkv = w["rwkv_mu"].shape[1]
    c_b = w["conf_dw_b"].shape[1]
    c_c = w["s5_d"].shape[1]
    d_model = w["w_out"].shape[1]
    lora_w = w["rwkv_w_up"].shape[1]
    lora_a = w["rwkv_a_up"].shape[1]
    assert lora_w + lora_a == LORA_PAIR and n_rwkv == 3 * c_a + LORA_PAIR + w["rwkv_g_up"].shape[1]
    n_grp, p_c = w["s5_lambda_re"].shape[1:]
    grp_per_slab = LANES // (c_c // n_grp)

    def slabs(blocks):
        t = blocks.transpose(0, 2, 1)
        t = t.reshape(n_grp // grp_per_slab, grp_per_slab, *t.shape[1:])
        return jax.vmap(_block_diag)(t).astype(BF16)

    assert c_a % MXU_DIM == 0
    ones_bd = _block_diag(jnp.ones((MXU_DIM // HEAD, HEAD, HEAD), F32)).astype(BF16)
    rwkv = dict(
        mu=row(w["rwkv_mu"][l]), w0=row(w["rwkv_w0"][l]), a0=row(w["rwkv_a0"][l]), k_k=row(w["rwkv_k_k"][l]),
        k_a=row(w["rwkv_k_a"][l]), r_k=row(w["rwkv_r_k"][l]),
        w_up=jnp.concatenate([w["rwkv_w_up"][l], jnp.zeros((lora_a, c_a), F32)], axis=0).astype(BF16),
        a_up=jnp.concatenate([jnp.zeros((lora_w, c_a), F32), w["rwkv_a_up"][l]], axis=0).astype(BF16),
        g_up=w["rwkv_g_up"][l].astype(BF16), ones_bd=ones_bd)
    conf = dict(dw_w=w["conf_dw_w"][l], dw_b=row(w["conf_dw_b"][l]), ln_g=row(w["conf_ln_g"][l]),
                ln_b=row(w["conf_ln_b"][l]))
    s5 = dict(
        lam_re=row(w["s5_lambda_re"][l]), lam_im=row(w["s5_lambda_im"][l]),
        log_dt=row(jnp.repeat(w["s5_log_dt"][l], p_c)),
        b_re=slabs(w["s5_b_re"][l]), b_im=slabs(w["s5_b_im"][l]),
        c_re=slabs(w["s5_c_re"][l]), c_im=slabs(w["s5_c_im"][l]),
        d=row(w["s5_d"][l]), w_glu=w["s5_w_glu"][l].astype(BF16))
    merge = dict(ln_g=row(w["rwkv_ln_g"][l]), ln_b=row(w["rwkv_ln_b"][l]), ones_bd=ones_bd,
                 rwkv_w_out=w["rwkv_w_out"][l].astype(BF16), conf_w_out=w["conf_w_out"][l].astype(BF16),
                 b_gate=row(w["b_gate"][l]), w_out=w["w_out"][l].astype(BF16))
    ffn = dict(norm_g=row(w["norm_ffn_g"][l]), w_up=w["ffn_w_up"][l].astype(BF16), dw_w=w["ffn_dw_w"][l],
               dw_b=row(w["ffn_dw_b"][l]), w_down=w["ffn_w_down"][l].astype(BF16))
    in_widths = (n_rwkv, 2 * c_b, c_c, w["b_gate"].shape[1])
    assert sum(in_widths) == w["w_in"].shape[2] and d_model * 3 == in_widths[3] and n_grp * p_c > 0
    return dict(norm_mix_g=row(w["norm_mix_g"][l]), w_in=w["w_in"][l].astype(BF16), in_widths=in_widths,
                rwkv=rwkv, conf=conf, s5=s5, merge=merge, ffn=ffn)


def kernel(x_prompt, x_sample, state_shift, state_wkv, state_conv, state_ssm_re, state_ssm_im, state_ffn_conv, norm_mix_g, w_in, b_gate, rwkv_mu, rwkv_w0, rwkv_w_up, rwkv_a0, rwkv_a_up, rwkv_g_up, rwkv_k_k, rwkv_k_a, rwkv_r_k, rwkv_ln_g, rwkv_ln_b, rwkv_w_out, conf_dw_w, conf_dw_b, conf_ln_g, conf_ln_b, conf_w_out, s5_lambda_re, s5_lambda_im, s5_log_dt, s5_b_re, s5_b_im, s5_c_re, s5_c_im, s5_d, s5_w_glu, w_out, norm_ffn_g, ffn_w_up, ffn_dw_w, ffn_dw_b, ffn_w_down, norm_final_g):
    w = dict(norm_mix_g=norm_mix_g, w_in=w_in, b_gate=b_gate, rwkv_mu=rwkv_mu, rwkv_w0=rwkv_w0,
             rwkv_w_up=rwkv_w_up, rwkv_a0=rwkv_a0, rwkv_a_up=rwkv_a_up, rwkv_g_up=rwkv_g_up, rwkv_k_k=rwkv_k_k,
             rwkv_k_a=rwkv_k_a, rwkv_r_k=rwkv_r_k, rwkv_ln_g=rwkv_ln_g, rwkv_ln_b=rwkv_ln_b,
             rwkv_w_out=rwkv_w_out, conf_dw_w=conf_dw_w, conf_dw_b=conf_dw_b, conf_ln_g=conf_ln_g,
             conf_ln_b=conf_ln_b, conf_w_out=conf_w_out, s5_lambda_re=s5_lambda_re, s5_lambda_im=s5_lambda_im,
             s5_log_dt=s5_log_dt, s5_b_re=s5_b_re, s5_b_im=s5_b_im, s5_c_re=s5_c_re, s5_c_im=s5_c_im, s5_d=s5_d,
             s5_w_glu=s5_w_glu, w_out=w_out, norm_ffn_g=norm_ffn_g, ffn_w_up=ffn_w_up, ffn_dw_w=ffn_dw_w,
             ffn_dw_b=ffn_dw_b, ffn_w_down=ffn_w_down)
    depth = w_in.shape[0]
    layers = [_prep_layer(l, w) for l in range(depth)]
    final_g = norm_final_g.reshape(1, -1)
    bp = x_prompt.shape[0]
    zeros_like_state = lambda s: jnp.zeros((depth, bp) + s.shape[2:], s.dtype)
    prompt_states = tuple(zeros_like_state(s) for s in
                          (state_shift, state_wkv, state_conv, state_ssm_re, state_ssm_im, state_ffn_conv))
    y_prompt, p_states = _run_trunk(x_prompt, prompt_states, layers, final_g)
    y_sample, s_states = _run_trunk(
        x_sample, (state_shift, state_wkv, state_conv, state_ssm_re, state_ssm_im, state_ffn_conv), layers, final_g)
    return (y_prompt, y_sample, *p_states, *s_states)
```

```python
import functools
import math

import jax
import jax.numpy as jnp
from jax import lax
from jax.experimental import pallas as pl
from jax.experimental.pallas import tpu as pltpu

F32 = jnp.float32
BF16 = jnp.bfloat16

VMEM_LIMIT_BYTES = 56 * 1024 * 1024
SUBLANES = 8
LANES = 128
MXU_DIM = 256

HEAD = 64
LORA_PAIR = 128
RMS_EPS = 1e-6
LN_EPS = 1e-5
GN_EPS = 64e-5
DECAY_SCALE = math.exp(-0.5)


def _params(*sem):
    return pltpu.CompilerParams(dimension_semantics=sem, vmem_limit_bytes=VMEM_LIMIT_BYTES)


def _const_spec(shape):
    zeros = (0,) * len(shape)
    return pl.BlockSpec(shape, lambda *_: zeros)


def _dot(a, b):
    return jnp.dot(a, b, preferred_element_type=F32)


def _head_sum(x, ones_bd):
    hi = x.astype(BF16)
    lo = (x - hi.astype(F32)).astype(BF16)
    wd = ones_bd.shape[0]
    return jnp.concatenate(
        [_dot(hi[:, s:s + wd], ones_bd) + _dot(lo[:, s:s + wd], ones_bd) for s in range(0, x.shape[1], wd)], axis=1)


def _rms_scale(x, g):
    ms = jnp.mean(x * x, axis=-1, keepdims=True)
    return x * lax.rsqrt(ms + RMS_EPS) * g


def _chain_geometry(bg, n_h):
    dup = max(1, LANES // (bg * n_h))
    nb = LANES // (n_h * dup)
    assert n_h % 2 == 0 and nb % SUBLANES == 0 and bg % nb == 0
    return dup, nb, HEAD // dup, bg // nb


def _swap_halves(x):
    return pltpu.roll(x, HEAD, 1)


def _to_chain_minor(x, dst_ref, bg, n_h):
    dup, nb, _, n_blk = _chain_geometry(bg, n_h)
    low = lax.broadcasted_iota(jnp.int32, (nb, LANES), 1) < HEAD
    for t0 in range(0, x.shape[0] // bg, 2):
        for c in range(n_blk):
            r0 = t0 * bg + c * nb
            r1 = r0 + bg
            pieces = []
            for q in range(n_h // 2):
                x0 = x[r0:r0 + nb, q * LANES:(q + 1) * LANES]
                x1 = x[r1:r1 + nb, q * LANES:(q + 1) * LANES]
                pieces.append(jnp.where(low, x0, _swap_halves(x1)))
                pieces.append(jnp.where(low, _swap_halves(x0), x1))
            tt = jnp.concatenate(pieces * dup, axis=0).T
            dst_ref[t0, :, c * LANES:(c + 1) * LANES] = tt[:HEAD]
            dst_ref[t0 + 1, :, c * LANES:(c + 1) * LANES] = tt[HEAD:]


def _from_chain_minor(src_ref, bg, n_h):
    dup, nb, _, n_blk = _chain_geometry(bg, n_h)
    n_step = src_ref.shape[0]
    low = lax.broadcasted_iota(jnp.int32, (nb, LANES), 1) < HEAD
    tiles = {}
    for t0 in range(0, n_step, 2):
        for c in range(n_blk):
            ys = []
            for t in (t0, t0 + 1):
                y = src_ref[t, :, c * LANES:(c + 1) * LANES]
                ys += [y] if dup == 1 else [y, _swap_halves(y)]
            wt = jnp.concatenate(ys, axis=0).T
            first, second = [], []
            for q in range(n_h // 2):
                even = wt[2 * q * nb:(2 * q + 1) * nb]
                odd = wt[(2 * q + 1) * nb:(2 * q + 2) * nb]
                first.append(jnp.where(low, even, _swap_halves(odd)))
                second.append(jnp.where(low, _swap_halves(even), odd))
            tiles[t0, c] = jnp.concatenate(first, axis=1)
            tiles[t0 + 1, c] = jnp.concatenate(second, axis=1)
    return jnp.concatenate([tiles[t, c] for t in range(n_step) for c in range(n_blk)], axis=0)


def _in_proj_kernel(x_ref, g_ref, w_ref, s0_ref, mu_ref, w0_ref, a0_ref, kk_ref, ka_ref, rk_ref, wup_ref, aup_ref,
                    gup_ref, ones_ref, pc_o, ps_o, pg_o, g_o, bonus_o, r_o, w_o, k_o, v_o, a_o, b_o, shift_o,
                    carry_ref, *, bg, c_a, bounds):
    @pl.when(pl.program_id(0) == 0)
    def _():
        carry_ref[...] = s0_ref[...]

    xn = _rms_scale(x_ref[...], g_ref[...]).astype(BF16)
    p = _dot(xn, w_ref[:, bounds[0][0]:bounds[0][1]])
    tm = p.shape[0]
    prev = carry_ref[...] if tm == bg else jnp.concatenate([carry_ref[...], p[:tm - bg]], axis=0)
    carry_ref[...] = p[tm - bg:]
    shift_o[...] = p[tm - bg:]
    xm = p + (prev - p) * mu_ref[...]
    r = xm[:, 0:c_a]
    k = xm[:, c_a:2 * c_a]
    v = xm[:, 2 * c_a:3 * c_a]
    xwa = xm[:, 3 * c_a:3 * c_a + LORA_PAIR]
    xg = xm[:, 3 * c_a + LORA_PAIR:]
    ones_bd = ones_ref[...]
    z = w0_ref[...] + _dot(jnp.tanh(xwa).astype(BF16), wup_ref[...])
    decay = jnp.exp(-DECAY_SCALE * jax.nn.sigmoid(z))
    a = jax.nn.sigmoid(a0_ref[...] + _dot(xwa.astype(BF16), aup_ref[...]))
    g = _dot(jax.nn.sigmoid(xg).astype(BF16), gup_ref[...])
    kk = k * kk_ref[...]
    kk = kk * lax.rsqrt(jnp.maximum(_head_sum(kk * kk, ones_bd), 1e-24))
    k_h = k * (1.0 + (a - 1.0) * ka_ref[...])
    g_o[...] = g
    bonus_o[...] = _head_sum(r * k_h * rk_ref[...], ones_bd) * v
    n_h = c_a // HEAD
    for val, dst_ref in ((r, r_o), (decay, w_o), (k_h, k_o), (v, v_o), (-kk, a_o), (kk * a, b_o)):
        _to_chain_minor(val, dst_ref, bg, n_h)
    for o_ref, (lo, hi) in zip((pc_o, ps_o, pg_o), bounds[1:]):
        o_ref[...] = _dot(xn, w_ref[:, lo:hi]).astype(o_ref.dtype)


def _in_proj(x, g, w_bf16, widths, shift0, prm, bg, tm):
    rows, d = x.shape
    c_a = prm["w0"].shape[1]
    bounds, lo = [], 0
    for wd in widths:
        bounds.append((lo, lo + wd))
        lo += wd
    consts = [prm["mu"], prm["w0"], prm["a0"], prm["k_k"], prm["k_a"], prm["r_k"], prm["w_up"], prm["a_up"],
              prm["g_up"], prm["ones_bd"]]
    out_w = list(widths[1:]) + [c_a] * 2
    n_chain = _chain_geometry(bg, c_a // HEAD)[3] * LANES
    outs = pl.pallas_call(
        functools.partial(_in_proj_kernel, bg=bg, c_a=c_a, bounds=tuple(bounds)),
        grid=(rows // tm,),
        in_specs=[pl.BlockSpec((tm, d), lambda i: (i, 0)), _const_spec(g.shape),
                  pl.BlockSpec(w_bf16.shape, lambda i: (0, 0), pipeline_mode=pl.Buffered(1)),
                  _const_spec(shift0.shape)] + [_const_spec(c.shape) for c in consts],
        out_specs=[pl.BlockSpec((tm, wd), lambda i: (i, 0)) for wd in out_w]
                  + [pl.BlockSpec((tm // bg, HEAD, n_chain), lambda i: (i, 0, 0))] * 6 + [_const_spec(shift0.shape)],
        out_shape=[jax.ShapeDtypeStruct((rows, wd), BF16 if n == 2 else F32) for n, wd in enumerate(out_w)]
                  + [jax.ShapeDtypeStruct((rows // bg, HEAD, n_chain), F32)] * 6
                  + [jax.ShapeDtypeStruct(shift0.shape, F32)],
        scratch_shapes=[pltpu.VMEM(shift0.shape, F32)],
        compiler_params=_params("arbitrary"),
        name="in_proj",
    )(x, g, w_bf16, shift0, *consts)
    return outs[0], outs[1], outs[2], outs[3], outs[4], outs[5:11], outs[11]


def _wkv_kernel(zr, zw, zk, zv, za, zb, s0_ref, zy_ref, st_ref, *, n_i, tc):
    dup = HEAD // n_i
    low_i = lax.broadcasted_iota(jnp.int32, (n_i, LANES), 1) < HEAD

    def row(ref, t, j):
        return ref[t, pl.ds(j, 1), :]

    @pl.when(pl.program_id(1) == 0)
    def _():
        st_ref[...] = s0_ref[...]

    sa0 = jnp.zeros((n_i, LANES), F32)
    for j in range(HEAD):
        sa0 = sa0 + st_ref[j] * row(za, 0, j)

    def step(t, sa):
        t_next = jnp.minimum(t + 1, tc - 1)
        v = zv[t] if dup == 1 else jnp.where(low_i, zv[t, :n_i, :], zv[t, n_i:, :])
        y = jnp.zeros((n_i, LANES), F32)
        sa_next = jnp.zeros((n_i, LANES), F32)
        for j in range(HEAD):
            s = st_ref[j] * row(zw, t, j) + sa * row(zb, t, j) + v * row(zk, t, j)
            st_ref[j] = s
            y = y + s * row(zr, t, j)
            sa_next = sa_next + s * row(za, t_next, j)
        zy_ref[t] = y
        return sa_next

    lax.fori_loop(0, tc, step, sa0)


def _wkv_group(operands, wkv0, bg, n_t):
    n_h = wkv0.shape[1]
    dup, nb, n_i, n_blk = _chain_geometry(bg, n_h)
    tc = min(n_t, 64)
    assert n_t % tc == 0
    s0 = wkv0.reshape(n_blk, nb, n_h, dup, n_i, HEAD).transpose(5, 4, 0, 3, 2, 1).reshape(HEAD, n_i, n_blk * LANES)
    seq = pl.BlockSpec((tc, HEAD, LANES), lambda c, t: (t, 0, c))
    state = pl.BlockSpec((HEAD, n_i, LANES), lambda c, t: (0, 0, c))
    return pl.pallas_call(
        functools.partial(_wkv_kernel, n_i=n_i, tc=tc),
        grid=(n_blk, n_t // tc),
        in_specs=[seq] * 6 + [state],
        out_specs=[pl.BlockSpec((tc, n_i, LANES), lambda c, t: (t, 0, c)), state],
        out_shape=[jax.ShapeDtypeStruct((n_t, n_i, n_blk * LANES), F32), jax.ShapeDtypeStruct(s0.shape, F32)],
        compiler_params=_params("arbitrary", "arbitrary"),
        name="wkv",
    )(*operands, s0)


def _wkv_states_from_chains(st, bg, n_h):
    depth, _, n_i, n_c = st.shape
    dup = HEAD // n_i
    nb = LANES // (n_h * dup)
    st = st.reshape(depth, HEAD, n_i, n_c // LANES, dup, n_h, nb).transpose(0, 3, 6, 5, 4, 2, 1)
    return st.reshape(depth, bg, n_h, HEAD, HEAD)


def _conf_kernel(p_ref, st0_ref, dww_ref, dwb_ref, lng_ref, lnb_ref, o_ref, st_o_ref, full_ref, *, bg, n_tap,
                 carry):
    tm, c2 = p_ref.shape
    c_b = c2 // 2
    halo = (n_tap - 1) * bg

    @pl.when(pl.program_id(0) == 0)
    def _():
        full_ref[0:halo, :] = st0_ref[...]

    p = p_ref[...]
    full_ref[halo:halo + tm, :] = p[:, :c_b] * jax.nn.sigmoid(p[:, c_b:])
    acc = jnp.zeros((tm, c_b), F32) + dwb_ref[...]
    for tap in range(n_tap):
        acc = acc + dww_ref[pl.ds(tap, 1), :] * full_ref[tap * bg:tap * bg + tm, :]
    mean = jnp.mean(acc, axis=-1, keepdims=True)
    cen = acc - mean
    var = jnp.mean(cen * cen, axis=-1, keepdims=True)
    n = cen * lax.rsqrt(var + LN_EPS) * lng_ref[...] + lnb_ref[...]
    o_ref[...] = (n * jax.nn.sigmoid(n)).astype(o_ref.dtype)
    tail = full_ref[tm:tm + halo, :]
    st_o_ref[...] = tail
    if carry:
        full_ref[0:halo, :] = tail


def _conformer(p_c, st0, prm, bg, tm):
    rows, c2 = p_c.shape
    c_b = c2 // 2
    n_tap = prm["dw_w"].shape[0]
    halo = (n_tap - 1) * bg
    consts = [prm["dw_w"], prm["dw_b"], prm["ln_g"], prm["ln_b"]]
    return pl.pallas_call(
        functools.partial(_conf_kernel, bg=bg, n_tap=n_tap, carry=rows > tm),
        grid=(rows // tm,),
        in_specs=[pl.BlockSpec((tm, c2), lambda i: (i, 0)), _const_spec(st0.shape)]
                 + [_const_spec(c.shape) for c in consts],
        out_specs=[pl.BlockSpec((tm, c_b), lambda i: (i, 0)), _const_spec((halo, c_b))],
        out_shape=[jax.ShapeDtypeStruct((rows, c_b), BF16), jax.ShapeDtypeStruct((halo, c_b), F32)],
        scratch_shapes=[pltpu.VMEM((halo + tm, c_b), F32)],
        compiler_params=_params("arbitrary"),
        name="conformer",
    )(p_c, st0, *consts)


def _s5_kernel(u_ref, h0r_ref, h0i_ref, lamr_ref, lami_ref, dt_ref, bre_ref, bim_ref, cre_ref, cim_ref, d_ref,
               wglu_ref, o_ref, hr_o, hi_o, sre_ref, sim_ref, coef_ref, hr_s, hi_s, *, bg):
    tm = u_ref.shape[0]
    n_state = sre_ref.shape[1]
    d_model = o_ref.shape[1]

    @pl.when(pl.program_id(0) == 0)
    def _():
        lr = lamr_ref[...]
        li = lami_ref[...]
        dt = jnp.exp(dt_ref[...])
        mag = jnp.exp(lr * dt)
        ab_re = mag * jnp.cos(li * dt)
        ab_im = mag * jnp.sin(li * dt)
        den = lr * lr + li * li
        coef_ref[0:1, :] = ab_re
        coef_ref[1:2, :] = ab_im
        coef_ref[2:3, :] = ((ab_re - 1.0) * lr + ab_im * li) / den
        coef_ref[3:4, :] = (ab_im * lr - (ab_re - 1.0) * li) / den
        hr_s[...] = h0r_ref[...]
        hi_s[...] = h0i_ref[...]

    u = u_ref[...]
    ub = u.astype(BF16)
    n_slab = bre_ref.shape[0]
    w_state = n_state // n_slab
    n_step = tm // bg
    ys = []
    for m in range(n_slab):
        um = ub[:, m * LANES:(m + 1) * LANES]
        st = slice(m * w_state, (m + 1) * w_state)
        bu_re = _dot(um, bre_ref[m])
        bu_im = _dot(um, bim_ref[m])
        q_re = coef_ref[2:3, st]
        q_im = coef_ref[3:4, st]
        sre_ref[:, st] = q_re * bu_re - q_im * bu_im
        sim_ref[:, st] = q_re * bu_im + q_im * bu_re
        ab_re = jnp.broadcast_to(coef_ref[0:1, st], (SUBLANES, w_state))
        ab_im = jnp.broadcast_to(coef_ref[1:2, st], (SUBLANES, w_state))
        for r0 in range(0, bg, SUBLANES):
            s_re = hr_s[r0:r0 + SUBLANES, st]
            s_im = hi_s[r0:r0 + SUBLANES, st]
            for t in range(n_step):
                rows = slice(t * bg + r0, t * bg + r0 + SUBLANES)
                s_re, s_im = (ab_re * s_re - ab_im * s_im + sre_ref[rows, st],
                              ab_re * s_im + ab_im * s_re + sim_ref[rows, st])
                sre_ref[rows, st] = s_re
                sim_ref[rows, st] = s_im
            hr_s[r0:r0 + SUBLANES, st] = s_re
            hi_s[r0:r0 + SUBLANES, st] = s_im
        ys.append(_dot(sre_ref[:, st].astype(BF16), cre_ref[m]) - _dot(sim_ref[:, st].astype(BF16), cim_ref[m]))

    hr_o[...] = hr_s[...]
    hi_o[...] = hi_s[...]
    y = jnp.concatenate(ys, axis=1) + d_ref[...] * u
    z = _dot(y.astype(BF16), wglu_ref[...])
    o_ref[...] = z[:, :d_model] * jax.nn.sigmoid(z[:, d_model:])


def _s5(u, h0r, h0i, prm, bg, tm):
    rows, c_c = u.shape
    n_state = h0r.shape[1]
    d_model = prm["w_glu"].shape[1] // 2
    consts = [prm["lam_re"], prm["lam_im"], prm["log_dt"], prm["b_re"], prm["b_im"], prm["c_re"], prm["c_im"],
              prm["d"], prm["w_glu"]]
    return pl.pallas_call(
        functools.partial(_s5_kernel, bg=bg),
        grid=(rows // tm,),
        in_specs=[pl.BlockSpec((tm, c_c), lambda i: (i, 0)), _const_spec(h0r.shape), _const_spec(h0i.shape)]
                 + [_const_spec(c.shape) for c in consts],
        out_specs=[pl.BlockSpec((tm, d_model), lambda i: (i, 0)), _const_spec(h0r.shape), _const_spec(h0i.shape)],
        out_shape=[jax.ShapeDtypeStruct((rows, d_model), F32), jax.ShapeDtypeStruct(h0r.shape, F32),
                   jax.ShapeDtypeStruct(h0i.shape, F32)],
        scratch_shapes=[pltpu.VMEM((tm, n_state), F32), pltpu.VMEM((tm, n_state), F32),
                        pltpu.VMEM((SUBLANES, n_state), F32), pltpu.VMEM(h0r.shape, F32),
                        pltpu.VMEM(h0i.shape, F32)],
        compiler_params=_params("arbitrary"),
        name="s5",
    )(u, h0r, h0i, *consts)


def _merge_kernel(zy_ref, bonus_ref, g_ref, yb_ref, yc_ref, pg_ref, x_ref, lng_ref, lnb_ref, ones_ref, woa_ref,
                  wob_ref, bgate_ref, wout_ref, o_ref, *, bg):
    d_model = x_ref.shape[1]
    ones_bd = ones_ref[...]
    y = _from_chain_minor(zy_ref, bg, bonus_ref.shape[1] // HEAD)
    cen = y - _head_sum(y, ones_bd) * (1.0 / HEAD)
    var = _head_sum(cen * cen, ones_bd) * (1.0 / HEAD)
    yn = cen * lax.rsqrt(var + GN_EPS) * lng_ref[...] + lnb_ref[...]
    y_a = _dot(((yn + bonus_ref[...]) * g_ref[...]).astype(BF16), woa_ref[...])
    y_b = _dot(yb_ref[...], wob_ref[...])
    gates = jax.nn.sigmoid(pg_ref[...].astype(F32) + bgate_ref[...])
    h = (gates[:, :d_model] * y_a + gates[:, d_model:2 * d_model] * y_b + gates[:, 2 * d_model:] * yc_ref[...])
    o_ref[...] = x_ref[...] + _dot(h.astype(BF16), wout_ref[...])


def _merge(zy, bonus, g, yb, yc, pg, x, prm, bg, tm):
    rows, d_model = x.shape
    acts = [bonus, g, yb, yc, pg, x]
    consts = [prm["ln_g"], prm["ln_b"], prm["ones_bd"], prm["rwkv_w_out"], prm["conf_w_out"], prm["b_gate"],
              prm["w_out"]]
    return pl.pallas_call(
        functools.partial(_merge_kernel, bg=bg),
        grid=(rows // tm,),
        in_specs=[pl.BlockSpec((tm // bg,) + zy.shape[1:], lambda i: (i, 0, 0))]
                 + [pl.BlockSpec((tm, a.shape[1]), lambda i: (i, 0)) for a in acts]
                 + [_const_spec(c.shape) for c in consts],
        out_specs=pl.BlockSpec((tm, d_model), lambda i: (i, 0)),
        out_shape=jax.ShapeDtypeStruct((rows, d_model), F32),
        compiler_params=_params("arbitrary"),
        name="merge",
    )(zy, *acts, *consts)


def _ffn_kernel(x_ref, st_ref, g_ref, wu_ref, dw_ref, db_ref, wd_ref, gf_ref, o_ref, carry_ref, *full_refs, bg, n_tap,
                d_ff, final_norm):
    tm = x_ref.shape[0]
    halo = (n_tap - 1) * bg

    @pl.when(pl.program_id(0) == 0)
    def _():
        carry_ref[...] = st_ref[...]

    x = x_ref[...]
    xn = _rms_scale(x, g_ref[...]).astype(BF16)
    acts = []
    for ci, lo in enumerate(range(0, d_ff, MXU_DIM)):
        wd = min(MXU_DIM, d_ff - lo)
        halves = []
        for h in range(2):
            cols = slice(h * d_ff + lo, h * d_ff + lo + wd)
            full_ref = full_refs[(2 * ci + h) % len(full_refs)]
            full_ref[0:halo, 0:wd] = carry_ref[:, cols]
            full_ref[halo:halo + tm, 0:wd] = _dot(xn, wu_ref[:, cols])
            c = jnp.zeros((tm, wd), F32) + db_ref[:, cols]
            for tap in range(n_tap):
                c = c + dw_ref[pl.ds(tap, 1), cols] * full_ref[tap * bg:tap * bg + tm, 0:wd]
            carry_ref[:, cols] = full_ref[tm:tm + halo, 0:wd]
            halves.append(c)
        acts.append((halves[0] * jax.nn.sigmoid(halves[0]) * halves[1]).astype(BF16))
    out = x + _dot(jnp.concatenate(acts, axis=1), wd_ref[...])
    o_ref[...] = _rms_scale(out, gf_ref[...]) if final_norm else out


FFN_STAGING_BUFFERS = 4


def _conv_ffn(x, st0, prm, final_g, bg, tm, final_norm):
    rows, d_model = x.shape
    d_ff = prm["w_down"].shape[0]
    n_tap = prm["dw_w"].shape[0]
    halo = (n_tap - 1) * bg
    resident = lambda a: pl.BlockSpec(a.shape, lambda i: (0,) * a.ndim, pipeline_mode=pl.Buffered(1))
    return pl.pallas_call(
        functools.partial(_ffn_kernel, bg=bg, n_tap=n_tap, d_ff=d_ff, final_norm=final_norm),
        grid=(rows // tm,),
        in_specs=[pl.BlockSpec((tm, d_model), lambda i: (i, 0)), resident(st0), _const_spec(prm["norm_g"].shape),
                  resident(prm["w_up"]), _const_spec(prm["dw_w"].shape), _const_spec(prm["dw_b"].shape),
                  resident(prm["w_down"]), _const_spec(final_g.shape)],
        out_specs=[pl.BlockSpec((tm, d_model), lambda i: (i, 0)), _const_spec(st0.shape)],
        out_shape=[jax.ShapeDtypeStruct((rows, d_model), F32), jax.ShapeDtypeStruct(st0.shape, F32)],
        scratch_shapes=[pltpu.VMEM((halo + tm, MXU_DIM), F32)] * FFN_STAGING_BUFFERS,
        compiler_params=_params("arbitrary"),
        name="conv_ffn",
    )(x, st0, prm["norm_g"], prm["w_up"], prm["dw_w"], prm["dw_b"], prm["w_down"], final_g)


def _tile(rows, bg, pref):
    tm = max(bg, min(pref, rows))
    while rows % tm or tm % bg:
        tm -= bg
    return tm


def _time_major(st):
    return st.transpose(1, 0, 2).reshape(-1, st.shape[-1])


def _run_trunk(x_bm, states, layers, final_g):
    bg, n_t, d_model = x_bm.shape
    rows = bg * n_t
    x = x_bm.transpose(1, 0, 2).reshape(rows, d_model)
    shift, wkv, conv, ssm_re, ssm_im, ffn_conv = states
    outs = ([], [], [], [], [], [])
    for l, prm in enumerate(layers):
        p_c, p_s, p_g, g, bonus, wkv_operands, shift_n = _in_proj(
            x, prm["norm_mix_g"], prm["w_in"], prm["in_widths"], shift[l], prm["rwkv"], bg,
            _tile(rows, 2 * bg, 256))
        zy, wkv_n = _wkv_group(wkv_operands, wkv[l], bg, n_t)
        yb, conv_n = _conformer(p_c, _time_major(conv[l]), prm["conf"], bg, _tile(rows, bg, 1024 if bg > 8 else 512))
        n_state = ssm_re.shape[2] * ssm_re.shape[3]
        yc, re_n, im_n = _s5(p_s, ssm_re[l].reshape(bg, n_state), ssm_im[l].reshape(bg, n_state), prm["s5"], bg,
                             _tile(rows, bg, 256 if bg > 8 else 512))
        x = _merge(zy, bonus, g, yb, yc, p_g, x, prm["merge"], bg, _tile(rows, 2 * bg, 512))
        x, ffn_n = _conv_ffn(x, _time_major(ffn_conv[l]), prm["ffn"], final_g, bg, _tile(rows, bg, 512),
                             final_norm=(l == len(layers) - 1))
        for lst, s in zip(outs, (shift_n, wkv_n, conv_n, re_n, im_n, ffn_n)):
            lst.append(s)
    y = x.reshape(n_t, bg, d_model).transpose(1, 0, 2)
    shift_n, wkv_n, conv_n, re_n, im_n, ffn_n = (jnp.stack(lst) for lst in outs)
    depth = len(layers)
    new_states = [shift_n, _wkv_states_from_chains(wkv_n, bg, wkv.shape[2]),
                  conv_n.reshape(depth, -1, bg, conv_n.shape[-1]).transpose(0, 2, 1, 3),
                  re_n.reshape(ssm_re.shape), im_n.reshape(ssm_im.shape),
                  ffn_n.reshape(depth, -1, bg, ffn_n.shape[-1]).transpose(0, 2, 1, 3)]
    return y, new_states


def _block_diag(blocks):
    n_g, a, b = blocks.shape
    eye = jnp.eye(n_g, dtype=blocks.dtype)
    return (eye[:, None, :, None] * blocks[:, :, None, :]).reshape(n_g * a, n_g * b)


def _prep_layer(l, w):
    row = lambda v: v.reshape(1, -1).astype(F32)
    c_a = w["rwkv_w0"].shape[1]
    n_rwkv = w["rwkv_mu"].shape[1]
    c_b = w["conf_dw_b"].shape[1]
    c_c = w["s5_d"].shape[1]
    d_model = w["w_out"].shape[1]
    lora_w = w["rwkv_w_up"].shape[1]
    lora_a = w["rwkv_a_up"].shape[1]
    assert lora_w + lora_a == LORA_PAIR and n_rwkv == 3 * c_a + LORA_PAIR + w["rwkv_g_up"].shape[1]
    n_grp, p_c = w["s5_lambda_re"].shape[1:]
    grp_per_slab = LANES // (c_c // n_grp)

    def slabs(blocks):
        t = blocks.transpose(0, 2, 1)
        t = t.reshape(n_grp // grp_per_slab, grp_per_slab, *t.shape[1:])
        return jax.vmap(_block_diag)(t).astype(BF16)

    assert c_a % MXU_DIM == 0
    ones_bd = _block_diag(jnp.ones((MXU_DIM // HEAD, HEAD, HEAD), F32)).astype(BF16)
    rwkv = dict(
        mu=row(w["rwkv_mu"][l]), w0=row(w["rwkv_w0"][l]), a0=row(w["rwkv_a0"][l]), k_k=row(w["rwkv_k_k"][l]),
        k_a=row(w["rwkv_k_a"][l]), r_k=row(w["rwkv_r_k"][l]),
        w_up=jnp.concatenate([w["rwkv_w_up"][l], jnp.zeros((lora_a, c_a), F32)], axis=0).astype(BF16),
        a_up=jnp.concatenate([jnp.zeros((lora_w, c_a), F32), w["rwkv_a_up"][l]], axis=0).astype(BF16),
        g_up=w["rwkv_g_up"][l].astype(BF16), ones_bd=ones_bd)
    conf = dict(dw_w=w["conf_dw_w"][l], dw_b=row(w["conf_dw_b"][l]), ln_g=row(w["conf_ln_g"][l]),
                ln_b=row(w["conf_ln_b"][l]))
    s5 = dict(
        lam_re=row(w["s5_lambda_re"][l]), lam_im=row(w["s5_lambda_im"][l]),
        log_dt=row(jnp.repeat(w["s5_log_dt"][l], p_c)),
        b_re=slabs(w["s5_b_re"][l]), b_im=slabs(w["s5_b_im"][l]),
        c_re=slabs(w["s5_c_re"][l]), c_im=slabs(w["s5_c_im"][l]),
        d=row(w["s5_d"][l]), w_glu=w["s5_w_glu"][l].astype(BF16))
    merge = dict(ln_g=row(w["rwkv_ln_g"][l]), ln_b=row(w["rwkv_ln_b"][l]), ones_bd=ones_bd,
                 rwkv_w_out=w["rwkv_w_out"][l].astype(BF16), conf_w_out=w["conf_w_out"][l].astype(BF16),
                 b_gate=row(w["b_gate"][l]), w_out=w["w_out"][l].astype(BF16))
    ffn = dict(norm_g=row(w["norm_ffn_g"][l]), w_up=w["ffn_w_up"][l].astype(BF16), dw_w=w["ffn_dw_w"][l],
               dw_b=row(w["ffn_dw_b"][l]), w_down=w["ffn_w_down"][l].astype(BF16))
    in_widths = (n_rwkv, 2 * c_b, c_c, w["b_gate"].shape[1])
    assert sum(in_widths) == w["w_in"].shape[2] and d_model * 3 == in_widths[3] and n_grp * p_c > 0
    return dict(norm_mix_g=row(w["norm_mix_g"][l]), w_in=w["w_in"][l].astype(BF16), in_widths=in_widths,
                rwkv=rwkv, conf=conf, s5=s5, merge=merge, ffn=ffn)


def kernel(x_prompt, x_sample, state_shift, state_wkv, state_conv, state_ssm_re, state_ssm_im, state_ffn_conv, norm_mix_g, w_in, b_gate, rwkv_mu, rwkv_w0, rwkv_w_up, rwkv_a0, rwkv_a_up, rwkv_g_up, rwkv_k_k, rwkv_k_a, rwkv_r_k, rwkv_ln_g, rwkv_ln_b, rwkv_w_out, conf_dw_w, conf_dw_b, conf_ln_g, conf_ln_b, conf_w_out, s5_lambda_re, s5_lambda_im, s5_log_dt, s5_b_re, s5_b_im, s5_c_re, s5_c_im, s5_d, s5_w_glu, w_out, norm_ffn_g, ffn_w_up, ffn_dw_w, ffn_dw_b, ffn_w_down, norm_final_g):
    w = dict(norm_mix_g=norm_mix_g, w_in=w_in, b_gate=b_gate, rwkv_mu=rwkv_mu, rwkv_w0=rwkv_w0,
             rwkv_w_up=rwkv_w_up, rwkv_a0=rwkv_a0, rwkv_a_up=rwkv_a_up, rwkv_g_up=rwkv_g_up, rwkv_k_k=rwkv_k_k,
             rwkv_k_a=rwkv_k_a, rwkv_r_k=rwkv_r_k, rwkv_ln_g=rwkv_ln_g, rwkv_ln_b=rwkv_ln_b,
             rwkv_w_out=rwkv_w_out, conf_dw_w=conf_dw_w, conf_dw_b=conf_dw_b, conf_ln_g=conf_ln_g,
             conf_ln_b=conf_ln_b, conf_w_out=conf_w_out, s5_lambda_re=s5_lambda_re, s5_lambda_im=s5_lambda_im,
             s5_log_dt=s5_log_dt, s5_b_re=s5_b_re, s5_b_im=s5_b_im, s5_c_re=s5_c_re, s5_c_im=s5_c_im, s5_d=s5_d,
             s5_w_glu=s5_w_glu, w_out=w_out, norm_ffn_g=norm_ffn_g, ffn_w_up=ffn_w_up, ffn_dw_w=ffn_dw_w,
             ffn_dw_b=ffn_dw_b, ffn_w_down=ffn_w_down)
    depth = w_in.shape[0]
    layers = [_prep_layer(l, w) for l in range(depth)]
    final_g = norm_final_g.reshape(1, -1)
    bp = x_prompt.shape[0]
    zeros_like_state = lambda s: jnp.zeros((depth, bp) + s.shape[2:], s.dtype)
    prompt_states = tuple(zeros_like_state(s) for s in
                          (state_shift, state_wkv, state_conv, state_ssm_re, state_ssm_im, state_ffn_conv))
    y_prompt, p_states = _run_trunk(x_prompt, prompt_states, layers, final_g)
    y_sample, s_states = _run_trunk(
        x_sample, (state_shift, state_wkv, state_conv, state_ssm_re, state_ssm_im, state_ffn_conv), layers, final_g)
    return (y_prompt, y_sample, *p_states, *s_states)
```

```python
import functools
import math

import jax
import jax.numpy as jnp
from jax import lax
from jax.experimental import pallas as pl
from jax.experimental.pallas import tpu as pltpu

F32 = jnp.float32
BF16 = jnp.bfloat16

VMEM_LIMIT_BYTES = 56 * 1024 * 1024
SUBLANES = 8
LANES = 128
MXU_DIM = 256

HEAD = 64
LORA_PAIR = 128
RMS_EPS = 1e-6
LN_EPS = 1e-5
GN_EPS = 64e-5
DECAY_SCALE = math.exp(-0.5)


def _params(*sem):
    return pltpu.CompilerParams(dimension_semantics=sem, vmem_limit_bytes=VMEM_LIMIT_BYTES)


def _const_spec(shape):
    zeros = (0,) * len(shape)
    return pl.BlockSpec(shape, lambda *_: zeros)


def _dot(a, b):
    return jnp.dot(a, b, preferred_element_type=F32)


def _head_sum(x, ones_bd):
    hi = x.astype(BF16)
    lo = (x - hi.astype(F32)).astype(BF16)
    wd = ones_bd.shape[0]
    return jnp.concatenate(
        [_dot(hi[:, s:s + wd], ones_bd) + _dot(lo[:, s:s + wd], ones_bd) for s in range(0, x.shape[1], wd)], axis=1)


def _rms_scale(x, g):
    ms = jnp.mean(x * x, axis=-1, keepdims=True)
    return x * lax.rsqrt(ms + RMS_EPS) * g


def _chain_geometry(bg, n_h):
    dup = max(1, LANES // (bg * n_h))
    nb = LANES // (n_h * dup)
    assert n_h % 2 == 0 and nb % SUBLANES == 0 and bg % nb == 0
    return dup, nb, HEAD // dup, bg // nb


def _swap_halves(x):
    return pltpu.roll(x, HEAD, 1)


def _to_chain_minor(x, dst_ref, bg, n_h):
    dup, nb, _, n_blk = _chain_geometry(bg, n_h)
    low = lax.broadcasted_iota(jnp.int32, (nb, LANES), 1) < HEAD
    for t0 in range(0, x.shape[0] // bg, 2):
        for c in range(n_blk):
            r0 = t0 * bg + c * nb
            r1 = r0 + bg
            pieces = []
            for q in range(n_h // 2):
                x0 = x[r0:r0 + nb, q * LANES:(q + 1) * LANES]
                x1 = x[r1:r1 + nb, q * LANES:(q + 1) * LANES]
                pieces.append(jnp.where(low, x0, _swap_halves(x1)))
                pieces.append(jnp.where(low, _swap_halves(x0), x1))
            tt = jnp.concatenate(pieces * dup, axis=0).T
            dst_ref[t0, :, c * LANES:(c + 1) * LANES] = tt[:HEAD]
            dst_ref[t0 + 1, :, c * LANES:(c + 1) * LANES] = tt[HEAD:]


def _from_chain_minor(src_ref, bg, n_h):
    dup, nb, _, n_blk = _chain_geometry(bg, n_h)
    n_step = src_ref.shape[0]
    low = lax.broadcasted_iota(jnp.int32, (nb, LANES), 1) < HEAD
    tiles = {}
    for t0 in range(0, n_step, 2):
        for c in range(n_blk):
            ys = []
            for t in (t0, t0 + 1):
                y = src_ref[t, :, c * LANES:(c + 1) * LANES]
                ys += [y] if dup == 1 else [y, _swap_halves(y)]
            wt = jnp.concatenate(ys, axis=0).T
            first, second = [], []
            for q in range(n_h // 2):
                even = wt[2 * q * nb:(2 * q + 1) * nb]
                odd = wt[(2 * q + 1) * nb:(2 * q + 2) * nb]
                first.append(jnp.where(low, even, _swap_halves(odd)))
                second.append(jnp.where(low, _swap_halves(even), odd))
            tiles[t0, c] = jnp.concatenate(first, axis=1)
            tiles[t0 + 1, c] = jnp.concatenate(second, axis=1)
    return jnp.concatenate([tiles[t, c] for t in range(n_step) for c in range(n_blk)], axis=0)


def _in_proj_kernel(x_ref, g_ref, w_ref, s0_ref, mu_ref, w0_ref, a0_ref, kk_ref, ka_ref, rk_ref, wup_ref, aup_ref,
                    gup_ref, ones_ref, pc_o, ps_o, pg_o, g_o, bonus_o, r_o, w_o, k_o, v_o, a_o, b_o, shift_o,
                    carry_ref, *, bg, c_a, bounds):
    @pl.when(pl.program_id(0) == 0)
    def _():
        carry_ref[...] = s0_ref[...]

    xn = _rms_scale(x_ref[...], g_ref[...]).astype(BF16)
    p = _dot(xn, w_ref[:, bounds[0][0]:bounds[0][1]])
    tm = p.shape[0]
    prev = carry_ref[...] if tm == bg else jnp.concatenate([carry_ref[...], p[:tm - bg]], axis=0)
    carry_ref[...] = p[tm - bg:]
    shift_o[...] = p[tm - bg:]
    xm = p + (prev - p) * mu_ref[...]
    r = xm[:, 0:c_a]
    k = xm[:, c_a:2 * c_a]
    v = xm[:, 2 * c_a:3 * c_a]
    xwa = xm[:, 3 * c_a:3 * c_a + LORA_PAIR]
    xg = xm[:, 3 * c_a + LORA_PAIR:]
    ones_bd = ones_ref[...]
    z = w0_ref[...] + _dot(jnp.tanh(xwa).astype(BF16), wup_ref[...])
    decay = jnp.exp(-DECAY_SCALE * jax.nn.sigmoid(z))
    a = jax.nn.sigmoid(a0_ref[...] + _dot(xwa.astype(BF16), aup_ref[...]))
    g = _dot(jax.nn.sigmoid(xg).astype(BF16), gup_ref[...])
    kk = k * kk_ref[...]
    kk = kk * lax.rsqrt(jnp.maximum(_head_sum(kk * kk, ones_bd), 1e-24))
    k_h = k * (1.0 + (a - 1.0) * ka_ref[...])
    g_o[...] = g
    bonus_o[...] = _head_sum(r * k_h * rk_ref[...], ones_bd) * v
    n_h = c_a // HEAD
    for val, dst_ref in ((r, r_o), (decay, w_o), (k_h, k_o), (v, v_o), (-kk, a_o), (kk * a, b_o)):
        _to_chain_minor(val, dst_ref, bg, n_h)
    for o_ref, (lo, hi) in zip((pc_o, ps_o, pg_o), bounds[1:]):
        o_ref[...] = _dot(xn, w_ref[:, lo:hi]).astype(o_ref.dtype)


def _in_proj(x, g, w_bf16, widths, shift0, prm, bg, tm):
    rows, d = x.shape
    c_a = prm["w0"].shape[1]
    bounds, lo = [], 0
    for wd in widths:
        bounds.append((lo, lo + wd))
        lo += wd
    consts = [prm["mu"], prm["w0"], prm["a0"], prm["k_k"], prm["k_a"], prm["r_k"], prm["w_up"], prm["a_up"],
              prm["g_up"], prm["ones_bd"]]
    out_w = list(widths[1:]) + [c_a] * 2
    n_chain = _chain_geometry(bg, c_a // HEAD)[3] * LANES
    outs = pl.pallas_call(
        functools.partial(_in_proj_kernel, bg=bg, c_a=c_a, bounds=tuple(bounds)),
        grid=(rows // tm,),
        in_specs=[pl.BlockSpec((tm, d), lambda i: (i, 0)), _const_spec(g.shape),
                  pl.BlockSpec(w_bf16.shape, lambda i: (0, 0), pipeline_mode=pl.Buffered(1)),
                  _const_spec(shift0.shape)] + [_const_spec(c.shape) for c in consts],
        out_specs=[pl.BlockSpec((tm, wd), lambda i: (i, 0)) for wd in out_w]
                  + [pl.BlockSpec((tm // bg, HEAD, n_chain), lambda i: (i, 0, 0))] * 6 + [_const_spec(shift0.shape)],
        out_shape=[jax.ShapeDtypeStruct((rows, wd), BF16 if n == 2 else F32) for n, wd in enumerate(out_w)]
                  + [jax.ShapeDtypeStruct((rows // bg, HEAD, n_chain), F32)] * 6
                  + [jax.ShapeDtypeStruct(shift0.shape, F32)],
        scratch_shapes=[pltpu.VMEM(shift0.shape, F32)],
        compiler_params=_params("arbitrary"),
        name="in_proj",
    )(x, g, w_bf16, shift0, *consts)
    return outs[0], outs[1], outs[2], outs[3], outs[4], outs[5:11], outs[11]


def _wkv_kernel(zr, zw, zk, zv, za, zb, s0_ref, zy_ref, st_ref, *, n_i, tc):
    dup = HEAD // n_i
    low_i = lax.broadcasted_iota(jnp.int32, (n_i, LANES), 1) < HEAD

    def row(ref, t, j):
        return ref[t, pl.ds(j, 1), :]

    @pl.when(pl.program_id(1) == 0)
    def _():
        st_ref[...] = s0_ref[...]

    sa0 = jnp.zeros((n_i, LANES), F32)
    for j in range(HEAD):
        sa0 = sa0 + st_ref[j] * row(za, 0, j)

    def step(t, sa):
        t_next = jnp.minimum(t + 1, tc - 1)
        v = zv[t] if dup == 1 else jnp.where(low_i, zv[t, :n_i, :], zv[t, n_i:, :])
        y = jnp.zeros((n_i, LANES), F32)
        sa_next = jnp.zeros((n_i, LANES), F32)
        for j in range(HEAD):
            s = st_ref[j] * row(zw, t, j) + sa * row(zb, t, j) + v * row(zk, t, j)
            st_ref[j] = s
            y = y + s * row(zr, t, j)
            sa_next = sa_next + s * row(za, t_next, j)
        zy_ref[t] = y
        return sa_next

    lax.fori_loop(0, tc, step, sa0)


def _wkv_group(operands, wkv0, bg, n_t):
    n_h = wkv0.shape[1]
    dup, nb, n_i, n_blk = _chain_geometry(bg, n_h)
    tc = min(n_t, 64)
    assert n_t % tc == 0
    s0 = wkv0.reshape(n_blk, nb, n_h, dup, n_i, HEAD).transpose(5, 4, 0, 3, 2, 1).reshape(HEAD, n_i, n_blk * LANES)
    seq = pl.BlockSpec((tc, HEAD, LANES), lambda c, t: (t, 0, c))
    state = pl.BlockSpec((HEAD, n_i, LANES), lambda c, t: (0, 0, c))
    return pl.pallas_call(
        functools.partial(_wkv_kernel, n_i=n_i, tc=tc),
        grid=(n_blk, n_t // tc),
        in_specs=[seq] * 6 + [state],
        out_specs=[pl.BlockSpec((tc, n_i, LANES), lambda c, t: (t, 0, c)), state],
        out_shape=[jax.ShapeDtypeStruct((n_t, n_i, n_blk * LANES), F32), jax.ShapeDtypeStruct(s0.shape, F32)],
        compiler_params=_params("arbitrary", "arbitrary"),
        name="wkv",
    )(*operands, s0)


def _wkv_states_from_chains(st, bg, n_h):
    depth, _, n_i, n_c = st.shape
    dup = HEAD // n_i
    nb = LANES // (n_h * dup)
    st = st.reshape(depth, HEAD, n_i, n_c // LANES, dup, n_h, nb).transpose(0, 3, 6, 5, 4, 2, 1)
    return st.reshape(depth, bg, n_h, HEAD, HEAD)


def _conf_kernel(p_ref, st0_ref, dww_ref, dwb_ref, lng_ref, lnb_ref, o_ref, st_o_ref, full_ref, *, bg, n_tap,
                 carry):
    tm, c2 = p_ref.shape
    c_b = c2 // 2
    halo = (n_tap - 1) * bg

    @pl.when(pl.program_id(0) == 0)
    def _():
        full_ref[0:halo, :] = st0_ref[...]

    p = p_ref[...]
    full_ref[halo:halo + tm, :] = p[:, :c_b] * jax.nn.sigmoid(p[:, c_b:])
    acc = jnp.zeros((tm, c_b), F32) + dwb_ref[...]
    for tap in range(n_tap):
        acc = acc + dww_ref[pl.ds(tap, 1), :] * full_ref[tap * bg:tap * bg + tm, :]
    mean = jnp.mean(acc, axis=-1, keepdims=True)
    cen = acc - mean
    var = jnp.mean(cen * cen, axis=-1, keepdims=True)
    n = cen * lax.rsqrt(var + LN_EPS) * lng_ref[...] + lnb_ref[...]
    o_ref[...] = (n * jax.nn.sigmoid(n)).astype(o_ref.dtype)
    tail = full_ref[tm:tm + halo, :]
    st_o_ref[...] = tail
    if carry:
        full_ref[0:halo, :] = tail


def _conformer(p_c, st0, prm, bg, tm):
    rows, c2 = p_c.shape
    c_b = c2 // 2
    n_tap = prm["dw_w"].shape[0]
    halo = (n_tap - 1) * bg
    consts = [prm["dw_w"], prm["dw_b"], prm["ln_g"], prm["ln_b"]]
    return pl.pallas_call(
        functools.partial(_conf_kernel, bg=bg, n_tap=n_tap, carry=rows > tm),
        grid=(rows // tm,),
        in_specs=[pl.BlockSpec((tm, c2), lambda i: (i, 0)), _const_spec(st0.shape)]
                 + [_const_spec(c.shape) for c in consts],
        out_specs=[pl.BlockSpec((tm, c_b), lambda i: (i, 0)), _const_spec((halo, c_b))],
        out_shape=[jax.ShapeDtypeStruct((rows, c_b), BF16), jax.ShapeDtypeStruct((halo, c_b), F32)],
        scratch_shapes=[pltpu.VMEM((halo + tm, c_b), F32)],
        compiler_params=_params("arbitrary"),
        name="conformer",
    )(p_c, st0, *consts)


def _s5_kernel(u_ref, h0r_ref, h0i_ref, lamr_ref, lami_ref, dt_ref, bre_ref, bim_ref, cre_ref, cim_ref, d_ref,
               wglu_ref, o_ref, hr_o, hi_o, sre_ref, sim_ref, coef_ref, hr_s, hi_s, *, bg):
    tm = u_ref.shape[0]
    n_state = sre_ref.shape[1]
    d_model = o_ref.shape[1]

    @pl.when(pl.program_id(0) == 0)
    def _():
        lr = lamr_ref[...]
        li = lami_ref[...]
        dt = jnp.exp(dt_ref[...])
        mag = jnp.exp(lr * dt)
        ab_re = mag * jnp.cos(li * dt)
        ab_im = mag * jnp.sin(li * dt)
        den = lr * lr + li * li
        coef_ref[0:1, :] = ab_re
        coef_ref[1:2, :] = ab_im
        coef_ref[2:3, :] = ((ab_re - 1.0) * lr + ab_im * li) / den
        coef_ref[3:4, :] = (ab_im * lr - (ab_re - 1.0) * li) / den
        hr_s[...] = h0r_ref[...]
        hi_s[...] = h0i_ref[...]

    u = u_ref[...]
    ub = u.astype(BF16)
    n_slab = bre_ref.shape[0]
    w_state = n_state // n_slab
    n_step = tm // bg
    ys = []
    for m in range(n_slab):
        um = ub[:, m * LANES:(m + 1) * LANES]
        st = slice(m * w_state, (m + 1) * w_state)
        bu_re = _dot(um, bre_ref[m])
        bu_im = _dot(um, bim_ref[m])
        q_re = coef_ref[2:3, st]
        q_im = coef_ref[3:4, st]
        sre_ref[:, st] = q_re * bu_re - q_im * bu_im
        sim_ref[:, st] = q_re * bu_im + q_im * bu_re
        ab_re = jnp.broadcast_to(coef_ref[0:1, st], (SUBLANES, w_state))
        ab_im = jnp.broadcast_to(coef_ref[1:2, st], (SUBLANES, w_state))
        for r0 in range(0, bg, SUBLANES):
            s_re = hr_s[r0:r0 + SUBLANES, st]
            s_im = hi_s[r0:r0 + SUBLANES, st]
            for t in range(n_step):
                rows = slice(t * bg + r0, t * bg + r0 + SUBLANES)
                s_re, s_im = (ab_re * s_re - ab_im * s_im + sre_ref[rows, st],
                              ab_re * s_im + ab_im * s_re + sim_ref[rows, st])
                sre_ref[rows, st] = s_re
                sim_ref[rows, st] = s_im
            hr_s[r0:r0 + SUBLANES, st] = s_re
            hi_s[r0:r0 + SUBLANES, st] = s_im
        ys.append(_dot(sre_ref[:, st].astype(BF16), cre_ref[m]) - _dot(sim_ref[:, st].astype(BF16), cim_ref[m]))

    hr_o[...] = hr_s[...]
    hi_o[...] = hi_s[...]
    y = jnp.concatenate(ys, axis=1) + d_ref[...] * u
    z = _dot(y.astype(BF16), wglu_ref[...])
    o_ref[...] = z[:, :d_model] * jax.nn.sigmoid(z[:, d_model:])


def _s5(u, h0r, h0i, prm, bg, tm):
    rows, c_c = u.shape
    n_state = h0r.shape[1]
    d_model = prm["w_glu"].shape[1] // 2
    consts = [prm["lam_re"], prm["lam_im"], prm["log_dt"], prm["b_re"], prm["b_im"], prm["c_re"], prm["c_im"],
              prm["d"], prm["w_glu"]]
    return pl.pallas_call(
        functools.partial(_s5_kernel, bg=bg),
        grid=(rows // tm,),
        in_specs=[pl.BlockSpec((tm, c_c), lambda i: (i, 0)), _const_spec(h0r.shape), _const_spec(h0i.shape)]
                 + [_const_spec(c.shape) for c in consts],
        out_specs=[pl.BlockSpec((tm, d_model), lambda i: (i, 0)), _const_spec(h0r.shape), _const_spec(h0i.shape)],
        out_shape=[jax.ShapeDtypeStruct((rows, d_model), F32), jax.ShapeDtypeStruct(h0r.shape, F32),
                   jax.ShapeDtypeStruct(h0i.shape, F32)],
        scratch_shapes=[pltpu.VMEM((tm, n_state), F32), pltpu.VMEM((tm, n_state), F32),
                        pltpu.VMEM((SUBLANES, n_state), F32), pltpu.VMEM(h0r.shape, F32),
                        pltpu.VMEM(h0i.shape, F32)],
        compiler_params=_params("arbitrary"),
        name="s5",
    )(u, h0r, h0i, *consts)


RING_SLOTS = 3


def _merge_kernel(zy_ref, bonus_ref, g_ref, yb_ref, yc_hbm, pg_hbm, x_hbm, lng_ref, lnb_ref, ones_ref, woa_ref,
                  wob_ref, bgate_ref, wout_ref, o_ref, yc_buf, pg_buf, x_buf, sems, *, bg):
    tm, d_model = o_ref.shape
    i = pl.program_id(0)
    n = pl.num_programs(0)

    def copies(step):
        slot = step % RING_SLOTS
        rows = pl.ds(pl.multiple_of(step * tm, tm), tm)
        return [pltpu.make_async_copy(src.at[rows, :], buf.at[slot], sems.at[k, slot])
                for k, (src, buf) in enumerate(((yc_hbm, yc_buf), (pg_hbm, pg_buf), (x_hbm, x_buf)))]

    @pl.when(i == 0)
    def _():
        for c in copies(0):
            c.start()

    @pl.when(jnp.logical_and(i == 0, n > 1))
    def _():
        for c in copies(1):
            c.start()

    @pl.when(i + 2 < n)
    def _():
        for c in copies(i + 2):
            c.start()

    for c in copies(i):
        c.wait()
    slot = i % RING_SLOTS
    yc_ref, pg_ref, x_ref = yc_buf.at[slot], pg_buf.at[slot], x_buf.at[slot]
    ones_bd = ones_ref[...]
    y = _from_chain_minor(zy_ref, bg, bonus_ref.shape[1] // HEAD)
    cen = y - _head_sum(y, ones_bd) * (1.0 / HEAD)
    var = _head_sum(cen * cen, ones_bd) * (1.0 / HEAD)
    yn = cen * lax.rsqrt(var + GN_EPS) * lng_ref[...] + lnb_ref[...]
    y_a = _dot(((yn + bonus_ref[...]) * g_ref[...]).astype(BF16), woa_ref[...])
    y_b = _dot(yb_ref[...], wob_ref[...])
    gates = jax.nn.sigmoid(pg_ref[...].astype(F32) + bgate_ref[...])
    h = (gates[:, :d_model] * y_a + gates[:, d_model:2 * d_model] * y_b + gates[:, 2 * d_model:] * yc_ref[...])
    o_ref[...] = x_ref[...] + _dot(h.astype(BF16), wout_ref[...])


def _merge(zy, bonus, g, yb, yc, pg, x, prm, bg, tm):
    rows, d_model = x.shape
    acts = [bonus, g, yb]
    ring = [yc, pg, x]
    consts = [prm["ln_g"], prm["ln_b"], prm["ones_bd"], prm["rwkv_w_out"], prm["conf_w_out"], prm["b_gate"],
              prm["w_out"]]
    return pl.pallas_call(
        functools.partial(_merge_kernel, bg=bg),
        grid=(rows // tm,),
        in_specs=[pl.BlockSpec((tm // bg,) + zy.shape[1:], lambda i: (i, 0, 0))]
                 + [pl.BlockSpec((tm, a.shape[1]), lambda i: (i, 0)) for a in acts]
                 + [pl.BlockSpec(memory_space=pl.ANY)] * len(ring)
                 + [_const_spec(c.shape) for c in consts],
        out_specs=pl.BlockSpec((tm, d_model), lambda i: (i, 0)),
        out_shape=jax.ShapeDtypeStruct((rows, d_model), F32),
        scratch_shapes=[pltpu.VMEM((RING_SLOTS, tm, a.shape[1]), a.dtype) for a in ring]
                       + [pltpu.SemaphoreType.DMA((len(ring), RING_SLOTS))],
        compiler_params=_params("arbitrary"),
        name="merge",
    )(zy, *acts, *ring, *consts)


def _ffn_kernel(x_ref, st_ref, g_ref, wu_ref, dw_ref, db_ref, wd_ref, gf_ref, o_ref, carry_ref, *full_refs, bg, n_tap,
                d_ff, final_norm):
    tm = x_ref.shape[0]
    halo = (n_tap - 1) * bg

    @pl.when(pl.program_id(0) == 0)
    def _():
        carry_ref[...] = st_ref[...]

    x = x_ref[...]
    xn = _rms_scale(x, g_ref[...]).astype(BF16)
    acts = []
    for ci, lo in enumerate(range(0, d_ff, MXU_DIM)):
        wd = min(MXU_DIM, d_ff - lo)
        halves = []
        for h in range(2):
            cols = slice(h * d_ff + lo, h * d_ff + lo + wd)
            full_ref = full_refs[(2 * ci + h) % len(full_refs)]
            full_ref[0:halo, 0:wd] = carry_ref[:, cols]
            full_ref[halo:halo + tm, 0:wd] = _dot(xn, wu_ref[:, cols])
            c = jnp.zeros((tm, wd), F32) + db_ref[:, cols]
            for tap in range(n_tap):
                c = c + dw_ref[pl.ds(tap, 1), cols] * full_ref[tap * bg:tap * bg + tm, 0:wd]
            carry_ref[:, cols] = full_ref[tm:tm + halo, 0:wd]
            halves.append(c)
        acts.append((halves[0] * jax.nn.sigmoid(halves[0]) * halves[1]).astype(BF16))
    out = x + _dot(jnp.concatenate(acts, axis=1), wd_ref[...])
    o_ref[...] = _rms_scale(out, gf_ref[...]) if final_norm else out


FFN_STAGING_BUFFERS = 4


def _conv_ffn(x, st0, prm, final_g, bg, tm, final_norm):
    rows, d_model = x.shape
    d_ff = prm["w_down"].shape[0]
    n_tap = prm["dw_w"].shape[0]
    halo = (n_tap - 1) * bg
    resident = lambda a: pl.BlockSpec(a.shape, lambda i: (0,) * a.ndim, pipeline_mode=pl.Buffered(1))
    return pl.pallas_call(
        functools.partial(_ffn_kernel, bg=bg, n_tap=n_tap, d_ff=d_ff, final_norm=final_norm),
        grid=(rows // tm,),
        in_specs=[pl.BlockSpec((tm, d_model), lambda i: (i, 0)), resident(st0), _const_spec(prm["norm_g"].shape),
                  resident(prm["w_up"]), _const_spec(prm["dw_w"].shape), _const_spec(prm["dw_b"].shape),
                  resident(prm["w_down"]), _const_spec(final_g.shape)],
        out_specs=[pl.BlockSpec((tm, d_model), lambda i: (i, 0)), _const_spec(st0.shape)],
        out_shape=[jax.ShapeDtypeStruct((rows, d_model), F32), jax.ShapeDtypeStruct(st0.shape, F32)],
        scratch_shapes=[pltpu.VMEM((halo + tm, MXU_DIM), F32)] * FFN_STAGING_BUFFERS,
        compiler_params=_params("arbitrary"),
        name="conv_ffn",
    )(x, st0, prm["norm_g"], prm["w_up"], prm["dw_w"], prm["dw_b"], prm["w_down"], final_g)


def _tile(rows, bg, pref):
    tm = max(bg, min(pref, rows))
    while rows % tm or tm % bg:
        tm -= bg
    return tm


def _time_major(st):
    return st.transpose(1, 0, 2).reshape(-1, st.shape[-1])


def _run_trunk(x_bm, states, layers, final_g):
    bg, n_t, d_model = x_bm.shape
    rows = bg * n_t
    x = x_bm.transpose(1, 0, 2).reshape(rows, d_model)
    shift, wkv, conv, ssm_re, ssm_im, ffn_conv = states
    outs = ([], [], [], [], [], [])
    for l, prm in enumerate(layers):
        p_c, p_s, p_g, g, bonus, wkv_operands, shift_n = _in_proj(
            x, prm["norm_mix_g"], prm["w_in"], prm["in_widths"], shift[l], prm["rwkv"], bg,
            _tile(rows, 2 * bg, 256))
        zy, wkv_n = _wkv_group(wkv_operands, wkv[l], bg, n_t)
        yb, conv_n = _conformer(p_c, _time_major(conv[l]), prm["conf"], bg, _tile(rows, bg, 1024 if bg > 8 else 512))
        n_state = ssm_re.shape[2] * ssm_re.shape[3]
        yc, re_n, im_n = _s5(p_s, ssm_re[l].reshape(bg, n_state), ssm_im[l].reshape(bg, n_state), prm["s5"], bg,
                             _tile(rows, bg, 256 if bg > 8 else 512))
        x = _merge(zy, bonus, g, yb, yc, p_g, x, prm["merge"], bg, _tile(rows, 2 * bg, 512))
        x, ffn_n = _conv_ffn(x, _time_major(ffn_conv[l]), prm["ffn"], final_g, bg, _tile(rows, bg, 512),
                             final_norm=(l == len(layers) - 1))
        for lst, s in zip(outs, (shift_n, wkv_n, conv_n, re_n, im_n, ffn_n)):
            lst.append(s)
    y = x.reshape(n_t, bg, d_model).transpose(1, 0, 2)
    shift_n, wkv_n, conv_n, re_n, im_n, ffn_n = (jnp.stack(lst) for lst in outs)
    depth = len(layers)
    new_states = [shift_n, _wkv_states_from_chains(wkv_n, bg, wkv.shape[2]),
                  conv_n.reshape(depth, -1, bg, conv_n.shape[-1]).transpose(0, 2, 1, 3),
                  re_n.reshape(ssm_re.shape), im_n.reshape(ssm_im.shape),
                  ffn_n.reshape(depth, -1, bg, ffn_n.shape[-1]).transpose(0, 2, 1, 3)]
    return y, new_states


def _block_diag(blocks):
    n_g, a, b = blocks.shape
    eye = jnp.eye(n_g, dtype=blocks.dtype)
    return (eye[:, None, :, None] * blocks[:, :, None, :]).reshape(n_g * a, n_g * b)


def _prep_layer(l, w):
    row = lambda v: v.reshape(1, -1).astype(F32)
    c_a = w["rwkv_w0"].shape[1]
    n_rwkv = w["rwkv_mu"].shape[1]
    c_b = w["conf_dw_b"].shape[1]
    c_c = w["s5_d"].shape[1]
    d_model = w["w_out"].shape[1]
    lora_w = w["rwkv_w_up"].shape[1]
    lora_a = w["rwkv_a_up"].shape[1]
    assert lora_w + lora_a == LORA_PAIR and n_rwkv == 3 * c_a + LORA_PAIR + w["rwkv_g_up"].shape[1]
    n_grp, p_c = w["s5_lambda_re"].shape[1:]
    grp_per_slab = LANES // (c_c // n_grp)

    def slabs(blocks):
        t = blocks.transpose(0, 2, 1)
        t = t.reshape(n_grp // grp_per_slab, grp_per_slab, *t.shape[1:])
        return jax.vmap(_block_diag)(t).astype(BF16)

    assert c_a % MXU_DIM == 0
    ones_bd = _block_diag(jnp.ones((MXU_DIM // HEAD, HEAD, HEAD), F32)).astype(BF16)
    rwkv = dict(
        mu=row(w["rwkv_mu"][l]), w0=row(w["rwkv_w0"][l]), a0=row(w["rwkv_a0"][l]), k_k=row(w["rwkv_k_k"][l]),
        k_a=row(w["rwkv_k_a"][l]), r_k=row(w["rwkv_r_k"][l]),
        w_up=jnp.concatenate([w["rwkv_w_up"][l], jnp.zeros((lora_a, c_a), F32)], axis=0).astype(BF16),
        a_up=jnp.concatenate([jnp.zeros((lora_w, c_a), F32), w["rwkv_a_up"][l]], axis=0).astype(BF16),
        g_up=w["rwkv_g_up"][l].astype(BF16), ones_bd=ones_bd)
    conf = dict(dw_w=w["conf_dw_w"][l], dw_b=row(w["conf_dw_b"][l]), ln_g=row(w["conf_ln_g"][l]),
                ln_b=row(w["conf_ln_b"][l]))
    s5 = dict(
        lam_re=row(w["s5_lambda_re"][l]), lam_im=row(w["s5_lambda_im"][l]),
        log_dt=row(jnp.repeat(w["s5_log_dt"][l], p_c)),
        b_re=slabs(w["s5_b_re"][l]), b_im=slabs(w["s5_b_im"][l]),
        c_re=slabs(w["s5_c_re"][l]), c_im=slabs(w["s5_c_im"][l]),
        d=row(w["s5_d"][l]), w_glu=w["s5_w_glu"][l].astype(BF16))
    merge = dict(ln_g=row(w["rwkv_ln_g"][l]), ln_b=row(w["rwkv_ln_b"][l]), ones_bd=ones_bd,
                 rwkv_w_out=w["rwkv_w_out"][l].astype(BF16), conf_w_out=w["conf_w_out"][l].astype(BF16),
                 b_gate=row(w["b_gate"][l]), w_out=w["w_out"][l].astype(BF16))
    ffn = dict(norm_g=row(w["norm_ffn_g"][l]), w_up=w["ffn_w_up"][l].astype(BF16), dw_w=w["ffn_dw_w"][l],
               dw_b=row(w["ffn_dw_b"][l]), w_down=w["ffn_w_down"][l].astype(BF16))
    in_widths = (n_rwkv, 2 * c_b, c_c, w["b_gate"].shape[1])
    assert sum(in_widths) == w["w_in"].shape[2] and d_model * 3 == in_widths[3] and n_grp * p_c > 0
    return dict(norm_mix_g=row(w["norm_mix_g"][l]), w_in=w["w_in"][l].astype(BF16), in_widths=in_widths,
                rwkv=rwkv, conf=conf, s5=s5, merge=merge, ffn=ffn)


def kernel(x_prompt, x_sample, state_shift, state_wkv, state_conv, state_ssm_re, state_ssm_im, state_ffn_conv, norm_mix_g, w_in, b_gate, rwkv_mu, rwkv_w0, rwkv_w_up, rwkv_a0, rwkv_a_up, rwkv_g_up, rwkv_k_k, rwkv_k_a, rwkv_r_k, rwkv_ln_g, rwkv_ln_b, rwkv_w_out, conf_dw_w, conf_dw_b, conf_ln_g, conf_ln_b, conf_w_out, s5_lambda_re, s5_lambda_im, s5_log_dt, s5_b_re, s5_b_im, s5_c_re, s5_c_im, s5_d, s5_w_glu, w_out, norm_ffn_g, ffn_w_up, ffn_dw_w, ffn_dw_b, ffn_w_down, norm_final_g):
    w = dict(norm_mix_g=norm_mix_g, w_in=w_in, b_gate=b_gate, rwkv_mu=rwkv_mu, rwkv_w0=rwkv_w0,
             rwkv_w_up=rwkv_w_up, rwkv_a0=rwkv_a0, rwkv_a_up=rwkv_a_up, rwkv_g_up=rwkv_g_up, rwkv_k_k=rwkv_k_k,
             rwkv_k_a=rwkv_k_a, rwkv_r_k=rwkv_r_k, rwkv_ln_g=rwkv_ln_g, rwkv_ln_b=rwkv_ln_b,
             rwkv_w_out=rwkv_w_out, conf_dw_w=conf_dw_w, conf_dw_b=conf_dw_b, conf_ln_g=conf_ln_g,
             conf_ln_b=conf_ln_b, conf_w_out=conf_w_out, s5_lambda_re=s5_lambda_re, s5_lambda_im=s5_lambda_im,
             s5_log_dt=s5_log_dt, s5_b_re=s5_b_re, s5_b_im=s5_b_im, s5_c_re=s5_c_re, s5_c_im=s5_c_im, s5_d=s5_d,
             s5_w_glu=s5_w_glu, w_out=w_out, norm_ffn_g=norm_ffn_g, ffn_w_up=ffn_w_up, ffn_dw_w=ffn_dw_w,
             ffn_dw_b=ffn_dw_b, ffn_w_down=ffn_w_down)
    depth = w_in.shape[0]
    layers = [_prep_layer(l, w) for l in range(depth)]
    final_g = norm_final_g.reshape(1, -1)
    bp = x_prompt.shape[0]
    zeros_like_state = lambda s: jnp.zeros((depth, bp) + s.shape[2:], s.dtype)
    prompt_states = tuple(zeros_like_state(s) for s in
                          (state_shift, state_wkv, state_conv, state_ssm_re, state_ssm_im, state_ffn_conv))
    y_prompt, p_states = _run_trunk(x_prompt, prompt_states, layers, final_g)
    y_sample, s_states = _run_trunk(
        x_sample, (state_shift, state_wkv, state_conv, state_ssm_re, state_ssm_im, state_ffn_conv), layers, final_g)
    return (y_prompt, y_sample, *p_states, *s_states)
```
